```python
import math
import jax, jax.numpy as jnp
from jax import lax
import numpy as np

D_MODEL = 2048
BATCH = 4
SEQ = 2048
DEPTH = 4
DEC_BATCH = 128
DEC_SEQ = 4
PAST_LEN = 16384
PAGE_SIZE = 128

N_MEM = 256
POOL_WIDTH = D_MODEL // 2
CONV_WIDTH = D_MODEL - POOL_WIDTH
POOL_WINDOWS = (2, 4, 8, 16)
N_POOL_GROUPS = len(POOL_WINDOWS)
POOL_GC = POOL_WIDTH // N_POOL_GROUPS
POOL_BUF = max(POOL_WINDOWS) - 1
CONV_K = 3
CONV_BUF = CONV_K - 1
N_CONV_HEADS = 8
IN_COLS = POOL_WIDTH + 3 * CONV_WIDTH
N_XHEADS = 4
XHEAD_DIM = D_MODEL // N_XHEADS
D_FF = 4 * D_MODEL
EPS = 1e-6

kernel_name = "hymba_pool_conv_memxattn_step"


def _rms(x, g):
    xf = x.astype(jnp.float32)
    y = xf * lax.rsqrt(jnp.mean(xf * xf, axis=-1, keepdims=True) + EPS)
    return (y * g.astype(jnp.float32)).astype(x.dtype)


def _pool_mixer(u_prev, u, pos0, w_pool, pool_scale):
    Bn, L, _ = u.shape
    ext = jnp.concatenate([u_prev, u], axis=1).astype(jnp.float32)
    cs = jnp.concatenate([jnp.zeros((Bn, 1, POOL_WIDTH), jnp.float32),
                          jnp.cumsum(ext, axis=1)], axis=1)
    end = cs[:, POOL_BUF + 1:POOL_BUF + 1 + L]
    pos = pos0 + jnp.arange(L)
    outs = []
    for g, w in enumerate(POOL_WINDOWS):
        sl = slice(g * POOL_GC, (g + 1) * POOL_GC)
        start = cs[:, POOL_BUF + 1 - w:POOL_BUF + 1 - w + L, sl]
        count = jnp.minimum(w, pos + 1).astype(jnp.float32)[None, :, None]
        mean = (end[..., sl] - start) / count
        d = (mean - u[..., sl].astype(jnp.float32)).astype(u.dtype)
        outs.append(jnp.einsum('blc,cd->bld', d, w_pool[g]))
    y = jnp.concatenate(outs, axis=-1) * pool_scale
    new_prev = jnp.concatenate([u_prev, u], axis=1)[:, -POOL_BUF:]
    return y, new_prev


def _conv_mixer(conv_prev, bg, cg, v, conv_w):
    L = v.shape[1]
    z = cg * v
    ext = jnp.concatenate([conv_prev, z], axis=1)
    c = conv_w[0] * ext[:, 0:L] + conv_w[1] * ext[:, 1:L + 1] + conv_w[2] * ext[:, 2:L + 2]
    return bg * c, ext[:, -CONV_BUF:]


def _mem_kv(mem, g_mem, w_k, w_v):
    m = _rms(mem, g_mem)
    Bn = mem.shape[0]
    k = (m @ w_k).reshape(Bn, N_MEM, N_XHEADS, XHEAD_DIM)
    v = (m @ w_v).reshape(Bn, N_MEM, N_XHEADS, XHEAD_DIM)
    return k, v


def _cross_attn(h, k, v, w_q, w_o):
    Bn, L, _ = h.shape
    q = (h @ w_q).reshape(Bn, L, N_XHEADS, XHEAD_DIM)
    s = jnp.einsum('blhd,bmhd->bhlm', q, k).astype(jnp.float32) / math.sqrt(XHEAD_DIM)
    p = jax.nn.softmax(s, axis=-1).astype(h.dtype)
    o = jnp.einsum('bhlm,bmhd->blhd', p, v).reshape(Bn, L, D_MODEL)
    return o @ w_o


def _layer(x, mem_k, mem_v, pool_prev, conv_prev, pos0,
           norm_mix, w_in, w_pool, pool_scale, conv_w, w_out,
           norm_attn, w_q, w_o, norm_mlp, w_up, w_down):
    h = _rms(x, norm_mix)
    proj = h @ w_in
    u = proj[..., :POOL_WIDTH]
    bg = proj[..., POOL_WIDTH:POOL_WIDTH + CONV_WIDTH]
    cg = proj[..., POOL_WIDTH + CONV_WIDTH:POOL_WIDTH + 2 * CONV_WIDTH]
    v = proj[..., POOL_WIDTH + 2 * CONV_WIDTH:]
    ya, new_pool = _pool_mixer(pool_prev, u, pos0, w_pool, pool_scale)
    yb, new_conv = _conv_mixer(conv_prev, bg, cg, v, conv_w)
    x = x + jnp.concatenate([ya, yb], axis=-1) @ w_out
    x = x + _cross_attn(_rms(x, norm_attn), mem_k, mem_v, w_q, w_o)
    hm = _rms(x, norm_mlp) @ w_up
    x = x + jnp.square(jax.nn.relu(hm)) @ w_down
    return x, new_pool, new_conv


def setup_inputs(seed: int = 0) -> dict:
    key = jax.random.key(seed)
    ks = jax.random.split(key, 24)
    f = jnp.float32
    n = lambda k, shape, s: jax.random.normal(k, shape, f) * s
    return {
        "x_prompt": n(ks[0], (BATCH, SEQ, D_MODEL), 1.0),
        "x_sample": n(ks[1], (DEC_BATCH, DEC_SEQ, D_MODEL), 1.0),
        "mem_prompt": n(ks[2], (BATCH, N_MEM, D_MODEL), 1.0),
        "cache_mem_k": n(ks[3], (DEPTH, DEC_BATCH, N_MEM, N_XHEADS, XHEAD_DIM), 1.0),
        "cache_mem_v": n(ks[4], (DEPTH, DEC_BATCH, N_MEM, N_XHEADS, XHEAD_DIM), 1.0),
        "state_pool": n(ks[5], (DEPTH, DEC_BATCH, POOL_BUF, POOL_WIDTH), 1.0),
        "state_conv": n(ks[6], (DEPTH, DEC_BATCH, CONV_BUF, CONV_WIDTH), 1.0),
        "norm_mix": 1.0 + n(ks[7], (DEPTH, D_MODEL), 0.05),
        "w_in": n(ks[8], (DEPTH, D_MODEL, IN_COLS), D_MODEL ** -0.5),
        "w_pool": n(ks[9], (DEPTH, N_POOL_GROUPS, POOL_GC, POOL_GC), POOL_GC ** -0.5),
        "pool_scale": 1.0 + n(ks[10], (DEPTH, POOL_WIDTH), 0.05),
        "conv_w": n(ks[11], (DEPTH, CONV_K, CONV_WIDTH), CONV_K ** -0.5),
        "w_out": n(ks[12], (DEPTH, D_MODEL, D_MODEL), D_MODEL ** -0.5),
        "norm_attn": 1.0 + n(ks[13], (DEPTH, D_MODEL), 0.05),
        "norm_mem": 1.0 + n(ks[14], (DEPTH, D_MODEL), 0.05),
        "w_q": n(ks[15], (DEPTH, D_MODEL, D_MODEL), D_MODEL ** -0.5),
        "w_k": n(ks[16], (DEPTH, D_MODEL, D_MODEL), D_MODEL ** -0.5),
        "w_v": n(ks[17], (DEPTH, D_MODEL, D_MODEL), D_MODEL ** -0.5),
        "w_o": n(ks[18], (DEPTH, D_MODEL, D_MODEL), D_MODEL ** -0.5),
        "norm_mlp": 1.0 + n(ks[19], (DEPTH, D_MODEL), 0.05),
        "w_up": n(ks[20], (DEPTH, D_MODEL, D_FF), D_MODEL ** -0.5),
        "w_down": n(ks[21], (DEPTH, D_FF, D_MODEL), D_FF ** -0.5),
        "norm_final": 1.0 + n(ks[22], (D_MODEL,), 0.05),
    }


def reference(x_prompt, x_sample, mem_prompt, cache_mem_k, cache_mem_v, state_pool, state_conv,
              norm_mix, w_in, w_pool, pool_scale, conv_w, w_out,
              norm_attn, norm_mem, w_q, w_k, w_v, w_o,
              norm_mlp, w_up, w_down, norm_final):
    xp, xs = x_prompt, x_sample
    pool_p, conv_p, memk_p, memv_p, pool_s, conv_s = [], [], [], [], [], []
    for l in range(DEPTH):
        lw = (norm_mix[l], w_in[l], w_pool[l], pool_scale[l], conv_w[l], w_out[l],
              norm_attn[l], w_q[l], w_o[l], norm_mlp[l], w_up[l], w_down[l])
        mk, mv = _mem_kv(mem_prompt, norm_mem[l], w_k[l], w_v[l])
        zp = jnp.zeros((xp.shape[0], POOL_BUF, POOL_WIDTH), xp.dtype)
        zc = jnp.zeros((xp.shape[0], CONV_BUF, CONV_WIDTH), xp.dtype)
        xp, npool, nconv = _layer(xp, mk, mv, zp, zc, 0, *lw)
        pool_p.append(npool); conv_p.append(nconv); memk_p.append(mk); memv_p.append(mv)
        xs, spool, sconv = _layer(xs, cache_mem_k[l], cache_mem_v[l], state_pool[l], state_conv[l],
                                  PAST_LEN, *lw)
        pool_s.append(spool); conv_s.append(sconv)
    y_prompt = _rms(xp, norm_final)
    y_sample = _rms(xs, norm_final)
    return (y_prompt, y_sample,
            jnp.stack(pool_p), jnp.stack(conv_p), jnp.stack(memk_p), jnp.stack(memv_p),
            jnp.stack(pool_s), jnp.stack(conv_s))
```

```python
import functools
import math

import jax
import jax.numpy as jnp
from jax import lax
from jax.experimental import pallas as pl
from jax.experimental.pallas import tpu as pltpu

F32 = jnp.float32
BF16 = jnp.bfloat16

PAST_LEN = 16384
POOL_WINDOWS = (2, 4, 8, 16)
CONV_K = 3
N_XHEADS = 4
EPS = 1e-6

V7X_VMEM_BYTES = 64 * 1024 * 1024
VMEM_LIMIT_BYTES = 56 * 1024 * 1024
SUBLANES_F32 = 8

ROW_TILE = 512


def _cparams(*sem):
    return pltpu.CompilerParams(dimension_semantics=sem, vmem_limit_bytes=VMEM_LIMIT_BYTES)


def _rms_rows(x, g):
    ms = jnp.mean(x * x, axis=-1, keepdims=True)
    return x * lax.rsqrt(ms + EPS) * g


def _rms_kernel(x_ref, g_ref, o_ref):
    o_ref[...] = _rms_rows(x_ref[...], g_ref[...]).astype(o_ref.dtype)


def _rms(x, g, out_dtype=BF16):
    rows, d = x.shape
    return pl.pallas_call(
        _rms_kernel,
        grid=(rows // ROW_TILE,),
        in_specs=[pl.BlockSpec((ROW_TILE, d), lambda i: (i, 0)),
                  pl.BlockSpec((1, d), lambda i: (0, 0))],
        out_specs=pl.BlockSpec((ROW_TILE, d), lambda i: (i, 0)),
        out_shape=jax.ShapeDtypeStruct((rows, d), out_dtype),
        compiler_params=_cparams("parallel"),
        name="rms",
    )(x, g.reshape(1, d))


def _mm_kernel(a_ref, w_ref, *rest, has_res, has_norm, emit_out):
    rest = list(rest)
    res_ref = rest.pop(0) if has_res else None
    g_ref = rest.pop(0) if has_norm else None
    o_ref = rest.pop(0) if emit_out else None
    h_ref = rest.pop(0) if has_norm else None
    acc = jnp.dot(a_ref[...], w_ref[...], preferred_element_type=F32)
    if has_res:
        acc = acc + res_ref[...]
    if emit_out:
        o_ref[...] = acc.astype(o_ref.dtype)
    if has_norm:
        h_ref[...] = _rms_rows(acc, g_ref[...]).astype(h_ref.dtype)


def _mm(a, w, *, res=None, norm_g=None, out_dtype=F32, norm_dtype=BF16, emit_out=True, tn=None):
    m, k = a.shape
    n = w.shape[1]
    tn = n if tn is None else tn
    assert m % ROW_TILE == 0 and n % tn == 0
    assert norm_g is None or tn == n
    in_specs = [pl.BlockSpec((ROW_TILE, k), lambda j, i: (i, 0)),
                pl.BlockSpec((k, tn), lambda j, i: (0, j))]
    args = [a, w]
    if res is not None:
        in_specs.append(pl.BlockSpec((ROW_TILE, tn), lambda j, i: (i, j)))
        args.append(res)
    if norm_g is not None:
        in_specs.append(pl.BlockSpec((1, n), lambda j, i: (0, 0)))
        args.append(norm_g.reshape(1, n))
    out_specs, out_shape = [], []
    if emit_out:
        out_specs.append(pl.BlockSpec((ROW_TILE, tn), lambda j, i: (i, j)))
        out_shape.append(jax.ShapeDtypeStruct((m, n), out_dtype))
    if norm_g is not None:
        out_specs.append(pl.BlockSpec((ROW_TILE, n), lambda j, i: (i, 0)))
        out_shape.append(jax.ShapeDtypeStruct((m, n), norm_dtype))
    outs = pl.pallas_call(
        functools.partial(_mm_kernel, has_res=res is not None, has_norm=norm_g is not None,
                          emit_out=emit_out),
        grid=(n // tn, m // ROW_TILE),
        in_specs=in_specs,
        out_specs=out_specs,
        out_shape=out_shape,
        compiler_params=_cparams("parallel", "parallel"),
        name="mm",
    )(*args)
    return outs[0] if len(outs) == 1 else tuple(outs)


def _memkv_kernel(m_ref, g_ref, wk_ref, wv_ref, k_ref, v_ref):
    h = _rms_rows(m_ref[...], g_ref[0]).astype(BF16)
    k_ref[0] = jnp.dot(h, wk_ref[0], preferred_element_type=F32)
    v_ref[0] = jnp.dot(h, wv_ref[0], preferred_element_type=F32)


def _mem_kv(mem, norm_mem, wk, wv):
    rows, d = mem.shape
    depth = wk.shape[0]
    tm = 256
    out = jax.ShapeDtypeStruct((depth, rows, d), F32)
    return pl.pallas_call(
        _memkv_kernel,
        grid=(depth, rows // tm),
        in_specs=[pl.BlockSpec((tm, d), lambda l, i: (i, 0)),
                  pl.BlockSpec((1, 1, d), lambda l, i: (l, 0, 0)),
                  pl.BlockSpec((1, d, d), lambda l, i: (l, 0, 0)),
                  pl.BlockSpec((1, d, d), lambda l, i: (l, 0, 0))],
        out_specs=[pl.BlockSpec((1, tm, d), lambda l, i: (l, i, 0)),
                   pl.BlockSpec((1, tm, d), lambda l, i: (l, i, 0))],
        out_shape=[out, out],
        compiler_params=_cparams("parallel", "parallel"),
        name="mem_kv",
    )(mem, norm_mem.reshape(depth, 1, d), wk, wv)


U_HALO = 16
Z_HALO = 8


def _mixer_kernel(proj_ref, uh_ref, ch_ref, vh_ref, sp_ref, sc_ref, wp_ref, ps_ref, cw_ref,
                  y_ref, zt_ref, zs_ref, eu_ref, ez_ref, d_ref,
                  *, n_prompt_tiles, tiles_per_seq, dec_batch, dec_seq):
    tm = ROW_TILE
    i = pl.program_id(0)
    pw = d_ref.shape[1]
    gc = pw // len(POOL_WINDOWS)
    cwid = ez_ref.shape[1]
    c_u, c_b, c_c, c_v = 0, pw, pw + cwid, pw + 2 * cwid
    cw0, cw1, cw2 = cw_ref[0:1, :], cw_ref[1:2, :], cw_ref[2:3, :]

    @pl.when(i < n_prompt_tiles)
    def _prompt():
        first = (i % tiles_per_seq) == 0
        eu_ref[0:U_HALO, :] = jnp.where(first, 0.0, uh_ref[...])
        eu_ref[U_HALO:U_HALO + tm, :] = proj_ref[:, c_u:c_u + pw]
        ez_ref[0:Z_HALO, :] = jnp.where(first, 0.0, ch_ref[...] * vh_ref[...])
        ez_ref[Z_HALO:Z_HALO + tm, :] = proj_ref[:, c_c:c_c + cwid] * proj_ref[:, c_v:c_v + cwid]
        zt_ref[0] = ez_ref[tm:tm + Z_HALO, :]
        pos = (i % tiles_per_seq) * tm + lax.broadcasted_iota(jnp.int32, (tm, 1), 0)
        for g, w in enumerate(POOL_WINDOWS):
            cols = slice(g * gc, (g + 1) * gc)
            u = eu_ref[U_HALO:U_HALO + tm, cols]
            s = u
            for j in range(1, w):
                s = s + eu_ref[U_HALO - j:U_HALO - j + tm, cols]
            count = jnp.minimum(w, pos + 1).astype(F32)
            d_ref[:, cols] = (s / count - u).astype(BF16)
        c = (cw0 * ez_ref[Z_HALO - 2:Z_HALO - 2 + tm, :]
             + cw1 * ez_ref[Z_HALO - 1:Z_HALO - 1 + tm, :]
             + cw2 * ez_ref[Z_HALO:Z_HALO + tm, :])
        y_ref[:, pw:pw + cwid] = (proj_ref[:, c_b:c_b + cwid] * c).astype(BF16)

    @pl.when(i >= n_prompt_tiles)
    def _sample():
        nb = dec_batch
        n_prev = sp_ref.shape[0]

        def ext_u(t, cols):
            if t < n_prev:
                return sp_ref[t, :, cols]
            return proj_ref[(t - n_prev) * nb:(t - n_prev + 1) * nb, c_u + cols.start:c_u + cols.stop]

        for g, w in enumerate(POOL_WINDOWS):
            cols = slice(g * gc, (g + 1) * gc)
            for l in range(dec_seq):
                t = n_prev + l
                s = ext_u(t, cols)
                for j in range(1, w):
                    s = s + ext_u(t - j, cols)
                count = float(min(w, PAST_LEN + l + 1))
                d_ref[l * nb:(l + 1) * nb, cols] = (s / count - ext_u(t, cols)).astype(BF16)

        n_cprev = sc_ref.shape[0]
        z = [sc_ref[t] for t in range(n_cprev)]
        for l in range(dec_seq):
            rows = slice(l * nb, (l + 1) * nb)
            z.append(proj_ref[rows, c_c:c_c + cwid] * proj_ref[rows, c_v:c_v + cwid])
        for l in range(dec_seq):
            rows = slice(l * nb, (l + 1) * nb)
            c = cw0 * z[l] + cw1 * z[l + 1] + cw2 * z[l + 2]
            y_ref[rows, pw:pw + cwid] = (proj_ref[rows, c_b:c_b + cwid] * c).astype(BF16)
        for t in range(n_cprev):
            zs_ref[t * nb:(t + 1) * nb, :] = z[dec_seq + t]
        zt_ref[0] = jnp.zeros(zt_ref.shape[1:], F32)

    for g in range(len(POOL_WINDOWS)):
        cols = slice(g * gc, (g + 1) * gc)
        ya = jnp.dot(d_ref[:, cols], wp_ref[g], preferred_element_type=F32)
        y_ref[:, cols] = (ya * ps_ref[:, cols]).astype(BF16)


def _mixer(proj, state_pool_tm, state_conv_tm, w_pool, pool_scale, conv_w, *, n_prompt_rows, seq):
    rows, in_cols = proj.shape
    tm = ROW_TILE
    pw = state_pool_tm.shape[2]
    cwid = state_conv_tm.shape[2]
    assert in_cols == pw + 3 * cwid
    n_tiles = rows // tm
    n_prompt_tiles = n_prompt_rows // tm
    dec_batch = state_pool_tm.shape[1]
    dec_seq = (rows - n_prompt_rows) // dec_batch
    assert n_tiles == n_prompt_tiles + 1 and dec_seq * dec_batch == tm and seq % tm == 0
    n_conv_prev = state_conv_tm.shape[0]
    uh, zh = tm // U_HALO, tm // Z_HALO
    c_blk = pw // cwid

    kernel = functools.partial(_mixer_kernel, n_prompt_tiles=n_prompt_tiles,
                               tiles_per_seq=seq // tm, dec_batch=dec_batch, dec_seq=dec_seq)
    return pl.pallas_call(
        kernel,
        grid=(n_tiles,),
        in_specs=[
            pl.BlockSpec((tm, in_cols), lambda i: (i, 0)),
            pl.BlockSpec((U_HALO, pw), lambda i: (jnp.maximum(i * uh - 1, 0), 0)),
            pl.BlockSpec((Z_HALO, cwid), lambda i: (jnp.maximum(i * zh - 1, 0), c_blk + 1)),
            pl.BlockSpec((Z_HALO, cwid), lambda i: (jnp.maximum(i * zh - 1, 0), c_blk + 2)),
            pl.BlockSpec(state_pool_tm.shape, lambda i: (0, 0, 0)),
            pl.BlockSpec(state_conv_tm.shape, lambda i: (0, 0, 0)),
            pl.BlockSpec(w_pool.shape, lambda i: (0, 0, 0)),
            pl.BlockSpec((1, pw), lambda i: (0, 0)),
            pl.BlockSpec(conv_w.shape, lambda i: (0, 0)),
        ],
        out_specs=[
            pl.BlockSpec((tm, pw + cwid), lambda i: (i, 0)),
            pl.BlockSpec((1, Z_HALO, cwid), lambda i: (i, 0, 0)),
            pl.BlockSpec((n_conv_prev * dec_batch, cwid), lambda i: (0, 0)),
        ],
        out_shape=[
            jax.ShapeDtypeStruct((rows, pw + cwid), BF16),
            jax.ShapeDtypeStruct((n_tiles, Z_HALO, cwid), F32),
            jax.ShapeDtypeStruct((n_conv_prev * dec_batch, cwid), F32),
        ],
        scratch_shapes=[
            pltpu.VMEM((U_HALO + tm, pw), F32),
            pltpu.VMEM((Z_HALO + tm, cwid), F32),
            pltpu.VMEM((tm, pw), BF16),
        ],
        compiler_params=_cparams("arbitrary"),
        name="mixer",
    )(proj, proj, proj, proj, state_pool_tm, state_conv_tm, w_pool, pool_scale.reshape(1, pw), conv_w)


def _softmax_rows(s):
    m = jnp.max(s, axis=-1, keepdims=True)
    e = jnp.exp(s - m)
    return e * (1.0 / jnp.sum(e, axis=-1, keepdims=True))


def _attn_prompt_kernel(q_ref, k_ref, v_ref, o_ref):
    d = q_ref.shape[1]
    hd = d // N_XHEADS
    scale = 1.0 / math.sqrt(hd)
    for h in range(N_XHEADS):
        cols = slice(h * hd, (h + 1) * hd)
        k = k_ref[0, :, cols].astype(BF16)
        v = v_ref[0, :, cols].astype(BF16)
        s = lax.dot_general(q_ref[:, cols], k, (((1,), (1,)), ((), ())),
                            preferred_element_type=F32) * scale
        p = _softmax_rows(s).astype(BF16)
        o_ref[:, cols] = jnp.dot(p, v, preferred_element_type=F32).astype(o_ref.dtype)


def _attn_prompt(q, mk, mv, layer, *, batch, seq):
    d = q.shape[1]
    n_mem = mk.shape[1] // batch
    tq = ROW_TILE
    tps = seq // tq
    return pl.pallas_call(
        _attn_prompt_kernel,
        grid=(batch, tps),
        in_specs=[pl.BlockSpec((tq, d), lambda b, t: (b * tps + t, 0)),
                  pl.BlockSpec((1, n_mem, d), lambda b, t: (layer, b, 0)),
                  pl.BlockSpec((1, n_mem, d), lambda b, t: (layer, b, 0))],
        out_specs=pl.BlockSpec((tq, d), lambda b, t: (b * tps + t, 0)),
        out_shape=jax.ShapeDtypeStruct((batch * seq, d), BF16),
        compiler_params=_cparams("parallel", "parallel"),
        name="attn_prompt",
    )(q, mk, mv)


SAMPLE_ATTN_BATCH_BLOCK = 4


def _attn_sample_kernel(q_ref, k_ref, v_ref, o_ref):
    bb, lq, d = q_ref.shape
    hd = d // N_XHEADS
    scale = 1.0 / math.sqrt(hd)
    pad = SUBLANES_F32 - lq
    rows = N_XHEADS * SUBLANES_F32
    head_of_row = lax.broadcasted_iota(jnp.int32, (rows, d), 0) // SUBLANES_F32
    head_of_col = lax.broadcasted_iota(jnp.int32, (rows, d), 1) // hd
    diag = head_of_row == head_of_col
    for i in range(bb):
        q8 = jnp.concatenate([q_ref[i], jnp.zeros((pad, d), F32)], axis=0)
        qbd = jnp.where(diag, jnp.concatenate([q8] * N_XHEADS, axis=0), 0.0).astype(BF16)
        k = k_ref[0, i].astype(BF16)
        v = v_ref[0, i].astype(BF16)
        s = lax.dot_general(qbd, k, (((1,), (1,)), ((), ())), preferred_element_type=F32) * scale
        p = _softmax_rows(s).astype(BF16)
        o_all = jnp.where(diag, jnp.dot(p, v, preferred_element_type=F32), 0.0)
        o8 = o_all[0:SUBLANES_F32]
        for h in range(1, N_XHEADS):
            o8 = o8 + o_all[h * SUBLANES_F32:(h + 1) * SUBLANES_F32]
        o_ref[i] = o8[0:lq].astype(o_ref.dtype)


def _attn_sample(q, cache_k, cache_v, layer):
    nb, lq, d = q.shape
    n_mem = cache_k.shape[2]
    bb = SAMPLE_ATTN_BATCH_BLOCK
    assert lq <= SUBLANES_F32 and nb % bb == 0
    return pl.pallas_call(
        _attn_sample_kernel,
        grid=(nb // bb,),
        in_specs=[pl.BlockSpec((bb, lq, d), lambda j: (j, 0, 0)),
                  pl.BlockSpec((1, bb, n_mem, d), lambda j: (layer, j, 0, 0)),
                  pl.BlockSpec((1, bb, n_mem, d), lambda j: (layer, j, 0, 0))],
        out_specs=pl.BlockSpec((bb, lq, d), lambda j: (j, 0, 0)),
        out_shape=jax.ShapeDtypeStruct((nb, lq, d), F32),
        compiler_params=_cparams("parallel"),
        name="attn_sample",
    )(q, cache_k, cache_v)


def _mlp_kernel(h_ref, x_ref, wu_ref, wd_ref, g_ref, xo_ref, ho_ref):
    j = pl.program_id(1)
    a = jnp.dot(h_ref[...], wu_ref[...], preferred_element_type=F32)
    a = jnp.square(jnp.maximum(a, 0.0)).astype(BF16)
    p = jnp.dot(a, wd_ref[...], preferred_element_type=F32)

    @pl.when(j == 0)
    def _():
        xo_ref[...] = x_ref[...] + p

    @pl.when(j > 0)
    def _():
        xo_ref[...] += p

    @pl.when(j == pl.num_programs(1) - 1)
    def _():
        ho_ref[...] = _rms_rows(xo_ref[...], g_ref[...]).astype(ho_ref.dtype)


MLP_HIDDEN_TILE = 1024


def _mlp(h, x, w_up, w_down, norm_g, norm_dtype):
    m, d = x.shape
    f = w_up.shape[1]
    tf = MLP_HIDDEN_TILE
    return pl.pallas_call(
        _mlp_kernel,
        grid=(m // ROW_TILE, f // tf),
        in_specs=[pl.BlockSpec((ROW_TILE, d), lambda i, j: (i, 0)),
                  pl.BlockSpec((ROW_TILE, d), lambda i, j: (i, 0)),
                  pl.BlockSpec((d, tf), lambda i, j: (0, j)),
                  pl.BlockSpec((tf, d), lambda i, j: (j, 0)),
                  pl.BlockSpec((1, d), lambda i, j: (0, 0))],
        out_specs=[pl.BlockSpec((ROW_TILE, d), lambda i, j: (i, 0)),
                   pl.BlockSpec((ROW_TILE, d), lambda i, j: (i, 0))],
        out_shape=[jax.ShapeDtypeStruct((m, d), F32),
                   jax.ShapeDtypeStruct((m, d), norm_dtype)],
        compiler_params=_cparams("parallel", "arbitrary"),
        name="mlp",
    )(h, x, w_up, w_down, norm_g.reshape(1, d))


def kernel(x_prompt, x_sample, mem_prompt, cache_mem_k, cache_mem_v, state_pool, state_conv,
           norm_mix, w_in, w_pool, pool_scale, conv_w, w_out, norm_attn, norm_mem, w_q, w_k, w_v,
           w_o, norm_mlp, w_up, w_down, norm_final):
    batch, seq, d = x_prompt.shape
    nb, lq, _ = x_sample.shape
    depth = w_in.shape[0]
    n_mem = mem_prompt.shape[1]
    pool_buf, pw = state_pool.shape[2], state_pool.shape[3]
    conv_buf, cwid = state_conv.shape[2], state_conv.shape[3]
    n_prompt_rows = batch * seq
    assert nb * lq == ROW_TILE and pool_buf < U_HALO and conv_buf == CONV_K - 1 <= Z_HALO
    assert cache_mem_k.shape[3] == N_XHEADS

    w_in_b, w_pool_b, w_out_b = w_in.astype(BF16), w_pool.astype(BF16), w_out.astype(BF16)
    w_q_b, w_k_b, w_v_b, w_o_b = (w.astype(BF16) for w in (w_q, w_k, w_v, w_o))
    w_up_b, w_down_b = w_up.astype(BF16), w_down.astype(BF16)

    x = jnp.concatenate([x_prompt.reshape(n_prompt_rows, d),
                         x_sample.transpose(1, 0, 2).reshape(lq * nb, d)], axis=0)
    cache_k = cache_mem_k.reshape(depth, nb, n_mem, d)
    cache_v = cache_mem_v.reshape(depth, nb, n_mem, d)

    mk, mv = _mem_kv(mem_prompt.reshape(batch * n_mem, d), norm_mem, w_k_b, w_v_b)

    h = _rms(x, norm_mix[0])
    pool_p, conv_p, pool_s, conv_s = [], [], [], []
    for l in range(depth):
        proj = _mm(h, w_in_b[l], tn=d)
        sp_tm = state_pool[l].transpose(1, 0, 2)
        sc_tm = state_conv[l].transpose(1, 0, 2)
        y, z_tail, z_s = _mixer(proj, sp_tm, sc_tm, w_pool_b[l], pool_scale[l], conv_w[l],
                                n_prompt_rows=n_prompt_rows, seq=seq)
        x, h = _mm(y, w_out_b[l], res=x, norm_g=norm_attn[l])
        u_p = proj[:n_prompt_rows, :pw].reshape(batch, seq, pw)
        pool_p.append(u_p[:, seq - pool_buf:])
        tiles_per_seq = seq // ROW_TILE
        zt = z_tail[:batch * tiles_per_seq].reshape(batch, tiles_per_seq, Z_HALO, cwid)
        conv_p.append(zt[:, -1, Z_HALO - conv_buf:])
        u_s = proj[n_prompt_rows:, :pw].reshape(lq, nb, pw).transpose(1, 0, 2)
        pool_s.append(jnp.concatenate([state_pool[l], u_s], axis=1)[:, -pool_buf:])
        conv_s.append(z_s.reshape(conv_buf, nb, cwid).transpose(1, 0, 2))
        q = _mm(h, w_q_b[l], out_dtype=BF16)
        o_p = _attn_prompt(q, mk, mv, l, batch=batch, seq=seq)
        q_s = q[n_prompt_rows:].astype(F32).reshape(lq, nb, d).transpose(1, 0, 2)
        o_s = _attn_sample(q_s, cache_k, cache_v, l)
        o = jnp.concatenate([o_p, o_s.transpose(1, 0, 2).reshape(lq * nb, d).astype(BF16)], axis=0)
        x, h = _mm(o, w_o_b[l], res=x, norm_g=norm_mlp[l])
        last = l == depth - 1
        g_next = norm_final if last else norm_mix[l + 1]
        x, h = _mlp(h, x, w_up_b[l], w_down_b[l], g_next, F32 if last else BF16)

    y_prompt = h[:n_prompt_rows].reshape(batch, seq, d)
    y_sample = h[n_prompt_rows:].reshape(lq, nb, d).transpose(1, 0, 2)
    mem_k = mk.reshape(depth, batch, n_mem, N_XHEADS, d // N_XHEADS)
    mem_v = mv.reshape(depth, batch, n_mem, N_XHEADS, d // N_XHEADS)
    return (y_prompt, y_sample, jnp.stack(pool_p), jnp.stack(conv_p), mem_k, mem_v,
            jnp.stack(pool_s), jnp.stack(conv_s))
```

```python
import functools
import math

import jax
import jax.numpy as jnp
from jax import lax
from jax.experimental import pallas as pl
from jax.experimental.pallas import tpu as pltpu

F32 = jnp.float32
BF16 = jnp.bfloat16

PAST_LEN = 16384
POOL_WINDOWS = (2, 4, 8, 16)
CONV_K = 3
N_XHEADS = 4
EPS = 1e-6

V7X_VMEM_BYTES = 64 * 1024 * 1024
VMEM_LIMIT_BYTES = V7X_VMEM_BYTES - 8 * 1024 * 1024
SUBLANES_F32 = 8

ROW_TILE = 512


def _cparams(*sem):
    return pltpu.CompilerParams(dimension_semantics=sem, vmem_limit_bytes=VMEM_LIMIT_BYTES)


def _rms_rows(x, g):
    ms = jnp.mean(x * x, axis=-1, keepdims=True)
    return x * lax.rsqrt(ms + EPS) * g


def _parts(x):
    return list(x) if isinstance(x, (list, tuple)) else [x]


def _part_specs(parts, cols, col_index, row_axis):
    specs, start = [], 0
    for p in parts:
        n = p.shape[0] // ROW_TILE
        assert p.shape[0] % ROW_TILE == 0

        def index_map(*ids, start=start, n=n):
            i = ids[row_axis]
            return (jnp.clip(i - start, 0, n - 1), col_index(*ids))

        mode = {"pipeline_mode": pl.Buffered(1)} if n == 1 and len(parts) > 1 else {}
        specs.append(pl.BlockSpec((ROW_TILE, cols), index_map, **mode))
        start += n
    return specs


def _read_parts(refs, i, first_part_tiles):
    if len(refs) == 1:
        return refs[0][...]
    return jnp.where(i < first_part_tiles, refs[0][...], refs[1][...])


def _rms_kernel(*refs, n_parts, first_part_tiles):
    x_refs, g_ref, o_ref = refs[:n_parts], refs[n_parts], refs[n_parts + 1]
    x = _read_parts(x_refs, pl.program_id(0), first_part_tiles)
    o_ref[...] = _rms_rows(x, g_ref[...]).astype(o_ref.dtype)


def _rms(x, g, out_dtype=BF16):
    parts = _parts(x)
    rows = sum(p.shape[0] for p in parts)
    d = parts[0].shape[1]
    return pl.pallas_call(
        functools.partial(_rms_kernel, n_parts=len(parts),
                          first_part_tiles=parts[0].shape[0] // ROW_TILE),
        grid=(rows // ROW_TILE,),
        in_specs=_part_specs(parts, d, lambda i: 0, 0) + [pl.BlockSpec((1, d), lambda i: (0, 0))],
        out_specs=pl.BlockSpec((ROW_TILE, d), lambda i: (i, 0)),
        out_shape=jax.ShapeDtypeStruct((rows, d), out_dtype),
        compiler_params=_cparams("parallel"),
        name="rms",
    )(*parts, g.reshape(1, d))


def _mm_kernel(*refs, n_a, n_res, has_norm, emit_out, first_part_tiles):
    refs = list(refs)
    a_refs = [refs.pop(0) for _ in range(n_a)]
    w_ref = refs.pop(0)
    res_refs = [refs.pop(0) for _ in range(n_res)]
    g_ref = refs.pop(0) if has_norm else None
    o_ref = refs.pop(0) if emit_out else None
    h_ref = refs.pop(0) if has_norm else None
    wb_ref = refs.pop(0)
    i = pl.program_id(1)

    @pl.when(i == 0)
    def _():
        wb_ref[...] = w_ref[0].astype(BF16)

    a = _read_parts(a_refs, i, first_part_tiles)
    acc = jnp.dot(a, wb_ref[...], preferred_element_type=F32)
    if n_res:
        acc = acc + _read_parts(res_refs, i, first_part_tiles)
    if emit_out:
        o_ref[...] = acc.astype(o_ref.dtype)
    if has_norm:
        h_ref[...] = _rms_rows(acc, g_ref[...]).astype(h_ref.dtype)


def _mm(a, w, layer, *, res=None, norm_g=None, out_dtype=F32, norm_dtype=BF16, emit_out=True,
        tn=None):
    a_parts = _parts(a)
    res_parts = _parts(res) if res is not None else []
    m = sum(p.shape[0] for p in a_parts)
    _, k, n = w.shape
    tn = n if tn is None else tn
    assert n % tn == 0 and (norm_g is None or tn == n)
    split = {ps[0].shape[0] // ROW_TILE for ps in (a_parts, res_parts) if len(ps) == 2}
    assert len(split) <= 1
    first_part_tiles = split.pop() if split else m // ROW_TILE
    in_specs = _part_specs(a_parts, k, lambda j, i: 0, 1)
    in_specs.append(pl.BlockSpec((1, k, tn), lambda j, i: (layer, 0, j),
                                 pipeline_mode=pl.Buffered(1)))
    in_specs += _part_specs(res_parts, tn, lambda j, i: j, 1)
    args = a_parts + [w] + res_parts
    if norm_g is not None:
        in_specs.append(pl.BlockSpec((1, n), lambda j, i: (0, 0)))
        args.append(norm_g.reshape(1, n))
    out_specs, out_shape = [], []
    if emit_out:
        out_specs.append(pl.BlockSpec((ROW_TILE, tn), lambda j, i: (i, j)))
        out_shape.append(jax.ShapeDtypeStruct((m, n), out_dtype))
    if norm_g is not None:
        out_specs.append(pl.BlockSpec((ROW_TILE, n), lambda j, i: (i, 0)))
        out_shape.append(jax.ShapeDtypeStruct((m, n), norm_dtype))
    outs = pl.pallas_call(
        functools.partial(_mm_kernel, n_a=len(a_parts), n_res=len(res_parts),
                          has_norm=norm_g is not None, emit_out=emit_out,
                          first_part_tiles=first_part_tiles),
        grid=(n // tn, m // ROW_TILE),
        in_specs=in_specs,
        out_specs=out_specs,
        out_shape=out_shape,
        scratch_shapes=[pltpu.VMEM((k, tn), BF16)],
        compiler_params=_cparams("arbitrary", "arbitrary"),
        name="mm",
    )(*args)
    return outs[0] if len(outs) == 1 else tuple(outs)


MEM_ROW_TILE = 256


def _mem_proj_kernel(m_ref, g_ref, w_ref, o_ref, ob_ref, wb_ref):
    @pl.when(pl.program_id(1) == 0)
    def _():
        wb_ref[...] = w_ref[0].astype(BF16)

    h = _rms_rows(m_ref[...], g_ref[0]).astype(BF16)
    kv = jnp.dot(h, wb_ref[...], preferred_element_type=F32)
    hd = o_ref.shape[-1]
    for head in range(N_XHEADS):
        o_ref[0, 0, :, head, :] = kv[:, head * hd:(head + 1) * hd]
        ob_ref[0, 0, head] = kv[:, head * hd:(head + 1) * hd].astype(BF16)


def _mem_proj(mem, norm_mem, w):
    batch, n_mem, d = mem.shape
    depth = w.shape[0]
    tm = MEM_ROW_TILE
    assert n_mem % tm == 0
    tiles_per_seq = n_mem // tm
    hd = d // N_XHEADS
    return pl.pallas_call(
        _mem_proj_kernel,
        grid=(depth, batch * tiles_per_seq),
        in_specs=[pl.BlockSpec((tm, d), lambda l, i: (i, 0)),
                  pl.BlockSpec((1, 1, d), lambda l, i: (l, 0, 0)),
                  pl.BlockSpec((1, d, d), lambda l, i: (l, 0, 0), pipeline_mode=pl.Buffered(1))],
        out_specs=[pl.BlockSpec((1, 1, tm, N_XHEADS, hd),
                                lambda l, i: (l, i // tiles_per_seq, i % tiles_per_seq, 0, 0)),
                   pl.BlockSpec((1, 1, N_XHEADS, tm, hd),
                                lambda l, i: (l, i // tiles_per_seq, 0, i % tiles_per_seq, 0))],
        out_shape=[jax.ShapeDtypeStruct((depth, batch, n_mem, N_XHEADS, hd), F32),
                   jax.ShapeDtypeStruct((depth, batch, N_XHEADS, n_mem, hd), BF16)],
        scratch_shapes=[pltpu.VMEM((d, d), BF16)],
        compiler_params=_cparams("arbitrary", "arbitrary"),
        name="mem_proj",
    )(mem.reshape(batch * n_mem, d), norm_mem.reshape(depth, 1, d), w)


U_HALO = 16
Z_HALO = 8


def _mixer_kernel(proj_ref, uh_ref, ch_ref, vh_ref, sp_ref, sc_ref, wp_ref, ps_ref, cw_ref,
                  y_ref, ut_ref, zt_ref, zs_ref, eu_ref, ez_ref, d_ref,
                  *, n_prompt_tiles, tiles_per_seq, dec_batch, dec_seq):
    tm = ROW_TILE
    i = pl.program_id(0)
    pw = d_ref.shape[1]
    gc = pw // len(POOL_WINDOWS)
    cwid = ez_ref.shape[1]
    c_u, c_b, c_c, c_v = 0, pw, pw + cwid, pw + 2 * cwid
    cw0, cw1, cw2 = cw_ref[0:1, :], cw_ref[1:2, :], cw_ref[2:3, :]
    ut_ref[0] = proj_ref[tm - U_HALO:tm, c_u:c_u + pw]

    @pl.when(i < n_prompt_tiles)
    def _prompt():
        first = (i % tiles_per_seq) == 0
        eu_ref[0:U_HALO, :] = jnp.where(first, 0.0, uh_ref[...])
        eu_ref[U_HALO:U_HALO + tm, :] = proj_ref[:, c_u:c_u + pw]
        ez_ref[0:Z_HALO, :] = jnp.where(first, 0.0, ch_ref[...] * vh_ref[...])
        ez_ref[Z_HALO:Z_HALO + tm, :] = proj_ref[:, c_c:c_c + cwid] * proj_ref[:, c_v:c_v + cwid]
        zt_ref[0] = ez_ref[tm:tm + Z_HALO, :]
        pos = (i % tiles_per_seq) * tm + lax.broadcasted_iota(jnp.int32, (tm, 1), 0)
        for g, w in enumerate(POOL_WINDOWS):
            cols = slice(g * gc, (g + 1) * gc)
            u = eu_ref[U_HALO:U_HALO + tm, cols]
            s = u
            for j in range(1, w):
                s = s + eu_ref[U_HALO - j:U_HALO - j + tm, cols]
            count = jnp.minimum(w, pos + 1).astype(F32)
            d_ref[:, cols] = (s / count - u).astype(BF16)
        c = (cw0 * ez_ref[Z_HALO - 2:Z_HALO - 2 + tm, :]
             + cw1 * ez_ref[Z_HALO - 1:Z_HALO - 1 + tm, :]
             + cw2 * ez_ref[Z_HALO:Z_HALO + tm, :])
        y_ref[:, pw:pw + cwid] = (proj_ref[:, c_b:c_b + cwid] * c).astype(BF16)

    @pl.when(i >= n_prompt_tiles)
    def _sample():
        nb = dec_batch
        n_prev = sp_ref.shape[0]

        def ext_u(t, cols):
            if t < n_prev:
                return sp_ref[t, :, cols]
            return proj_ref[(t - n_prev) * nb:(t - n_prev + 1) * nb, c_u + cols.start:c_u + cols.stop]

        for g, w in enumerate(POOL_WINDOWS):
            cols = slice(g * gc, (g + 1) * gc)
            for l in range(dec_seq):
                t = n_prev + l
                s = ext_u(t, cols)
                for j in range(1, w):
                    s = s + ext_u(t - j, cols)
                count = float(min(w, PAST_LEN + l + 1))
                d_ref[l * nb:(l + 1) * nb, cols] = (s / count - ext_u(t, cols)).astype(BF16)

        n_cprev = sc_ref.shape[0]
        z = [sc_ref[t] for t in range(n_cprev)]
        for l in range(dec_seq):
            rows = slice(l * nb, (l + 1) * nb)
            z.append(proj_ref[rows, c_c:c_c + cwid] * proj_ref[rows, c_v:c_v + cwid])
        for l in range(dec_seq):
            rows = slice(l * nb, (l + 1) * nb)
            c = cw0 * z[l] + cw1 * z[l + 1] + cw2 * z[l + 2]
            y_ref[rows, pw:pw + cwid] = (proj_ref[rows, c_b:c_b + cwid] * c).astype(BF16)
        for t in range(n_cprev):
            zs_ref[t * nb:(t + 1) * nb, :] = z[dec_seq + t]
        zt_ref[0] = jnp.zeros(zt_ref.shape[1:], F32)

    for g in range(len(POOL_WINDOWS)):
        cols = slice(g * gc, (g + 1) * gc)
        ya = jnp.dot(d_ref[:, cols], wp_ref[g].astype(BF16), preferred_element_type=F32)
        y_ref[:, cols] = (ya * ps_ref[:, cols]).astype(BF16)


def _mixer(proj, state_pool_tm, state_conv_tm, w_pool, pool_scale, conv_w, *, n_prompt_rows, seq):
    rows, in_cols = proj.shape
    tm = ROW_TILE
    pw = state_pool_tm.shape[2]
    cwid = state_conv_tm.shape[2]
    assert in_cols == pw + 3 * cwid
    n_tiles = rows // tm
    n_prompt_tiles = n_prompt_rows // tm
    dec_batch = state_pool_tm.shape[1]
    dec_seq = (rows - n_prompt_rows) // dec_batch
    assert n_tiles == n_prompt_tiles + 1 and dec_seq * dec_batch == tm and seq % tm == 0
    n_conv_prev = state_conv_tm.shape[0]
    uh, zh = tm // U_HALO, tm // Z_HALO
    c_blk = pw // cwid

    kernel = functools.partial(_mixer_kernel, n_prompt_tiles=n_prompt_tiles,
                               tiles_per_seq=seq // tm, dec_batch=dec_batch, dec_seq=dec_seq)
    return pl.pallas_call(
        kernel,
        grid=(n_tiles,),
        in_specs=[
            pl.BlockSpec((tm, in_cols), lambda i: (i, 0)),
            pl.BlockSpec((U_HALO, pw), lambda i: (jnp.maximum(i * uh - 1, 0), 0)),
            pl.BlockSpec((Z_HALO, cwid), lambda i: (jnp.maximum(i * zh - 1, 0), c_blk + 1)),
            pl.BlockSpec((Z_HALO, cwid), lambda i: (jnp.maximum(i * zh - 1, 0), c_blk + 2)),
            pl.BlockSpec(state_pool_tm.shape, lambda i: (0, 0, 0)),
            pl.BlockSpec(state_conv_tm.shape, lambda i: (0, 0, 0)),
            pl.BlockSpec(w_pool.shape, lambda i: (0, 0, 0)),
            pl.BlockSpec((1, pw), lambda i: (0, 0)),
            pl.BlockSpec(conv_w.shape, lambda i: (0, 0)),
        ],
        out_specs=[
            pl.BlockSpec((tm, pw + cwid), lambda i: (i, 0)),
            pl.BlockSpec((1, U_HALO, pw), lambda i: (i, 0, 0)),
            pl.BlockSpec((1, Z_HALO, cwid), lambda i: (i, 0, 0)),
            pl.BlockSpec((n_conv_prev * dec_batch, cwid), lambda i: (0, 0)),
        ],
        out_shape=[
            jax.ShapeDtypeStruct((rows, pw + cwid), BF16),
            jax.ShapeDtypeStruct((n_tiles, U_HALO, pw), F32),
            jax.ShapeDtypeStruct((n_tiles, Z_HALO, cwid), F32),
            jax.ShapeDtypeStruct((n_conv_prev * dec_batch, cwid), F32),
        ],
        scratch_shapes=[
            pltpu.VMEM((U_HALO + tm, pw), F32),
            pltpu.VMEM((Z_HALO + tm, cwid), F32),
            pltpu.VMEM((tm, pw), BF16),
        ],
        compiler_params=_cparams("arbitrary"),
        name="mixer",
    )(proj, proj, proj, proj, state_pool_tm, state_conv_tm, w_pool, pool_scale.reshape(1, pw), conv_w)


def _masked_softmax_rows(s, valid):
    s = jnp.where(valid, s, -jnp.inf)
    m = jnp.max(s, axis=-1, keepdims=True)
    e = jnp.exp(s - m)
    return e * (1.0 / jnp.sum(e, axis=-1, keepdims=True))


def _softmax_rows(s):
    m = jnp.max(s, axis=-1, keepdims=True)
    e = jnp.exp(s - m)
    return e * (1.0 / jnp.sum(e, axis=-1, keepdims=True))


def _attn_prompt_kernel(q_ref, k_ref, v_ref, o_ref):
    hd = k_ref.shape[-1]
    scale = 1.0 / math.sqrt(hd)
    for h in range(N_XHEADS):
        cols = slice(h * hd, (h + 1) * hd)
        s = lax.dot_general(q_ref[:, cols], k_ref[0, 0, h], (((1,), (1,)), ((), ())),
                            preferred_element_type=F32) * scale
        p = _softmax_rows(s).astype(BF16)
        o_ref[:, cols] = jnp.dot(p, v_ref[0, 0, h], preferred_element_type=F32).astype(o_ref.dtype)


def _attn_prompt(q, mk, mv, layer, *, batch, seq):
    d = q.shape[1]
    kv_block = (1, 1) + mk.shape[2:]
    tq = ROW_TILE
    tps = seq // tq
    return pl.pallas_call(
        _attn_prompt_kernel,
        grid=(batch, tps),
        in_specs=[pl.BlockSpec((tq, d), lambda b, t: (b * tps + t, 0)),
                  pl.BlockSpec(kv_block, lambda b, t: (layer, b, 0, 0, 0)),
                  pl.BlockSpec(kv_block, lambda b, t: (layer, b, 0, 0, 0))],
        out_specs=pl.BlockSpec((tq, d), lambda b, t: (b * tps + t, 0)),
        out_shape=jax.ShapeDtypeStruct((batch * seq, d), BF16),
        compiler_params=_cparams("parallel", "parallel"),
        name="attn_prompt",
    )(q, mk, mv)


SAMPLE_ATTN_BATCH_BLOCK = 4


def _attn_sample_kernel(q_ref, k_ref, v_ref, o_ref):
    bb, lq, d = q_ref.shape
    n_mem, n_heads, hd = k_ref.shape[2:]
    scale = 1.0 / math.sqrt(hd)
    n_rows = n_heads * SUBLANES_F32
    n_keys = n_mem * n_heads
    head_of_row = lax.broadcasted_iota(jnp.int32, (n_rows, n_keys), 0) // SUBLANES_F32
    head_of_key = lax.broadcasted_iota(jnp.int32, (n_rows, n_keys), 1) % n_heads
    valid = head_of_row == head_of_key
    for i in range(bb):
        q8 = jnp.concatenate([q_ref[i], jnp.zeros((SUBLANES_F32 - lq, d), F32)], axis=0)
        qh = jnp.concatenate([q8[:, h * hd:(h + 1) * hd] for h in range(n_heads)], axis=0)
        k2 = k_ref[0, i].reshape(n_keys, hd).astype(BF16)
        v2 = v_ref[0, i].reshape(n_keys, hd).astype(BF16)
        s = lax.dot_general(qh.astype(BF16), k2, (((1,), (1,)), ((), ())),
                            preferred_element_type=F32) * scale
        p = _masked_softmax_rows(s, valid).astype(BF16)
        o = jnp.dot(p, v2, preferred_element_type=F32)
        o_ref[i] = jnp.concatenate(
            [o[h * SUBLANES_F32:h * SUBLANES_F32 + lq] for h in range(n_heads)], axis=1)


def _attn_sample(q, cache_k, cache_v, layer):
    nb, lq, d = q.shape
    bb = SAMPLE_ATTN_BATCH_BLOCK
    kv_block = (1, bb) + cache_k.shape[2:]
    assert lq <= SUBLANES_F32 and nb % bb == 0
    return pl.pallas_call(
        _attn_sample_kernel,
        grid=(nb // bb,),
        in_specs=[pl.BlockSpec((bb, lq, d), lambda j: (j, 0, 0)),
                  pl.BlockSpec(kv_block, lambda j: (layer, j, 0, 0, 0)),
                  pl.BlockSpec(kv_block, lambda j: (layer, j, 0, 0, 0))],
        out_specs=pl.BlockSpec((bb, lq, d), lambda j: (j, 0, 0)),
        out_shape=jax.ShapeDtypeStruct((nb, lq, d), F32),
        compiler_params=_cparams("parallel"),
        name="attn_sample",
    )(q, cache_k, cache_v)


def _mlp_kernel(h_ref, x_ref, wu_ref, wd_ref, g_ref, *rest, split_norm_out, first_part_tiles):
    acc_ref = rest[-1]
    i, j = pl.program_id(0), pl.program_id(1)

    @pl.when(j == 0)
    def _():
        acc_ref[...] = x_ref[...]

    a = jnp.dot(h_ref[...], wu_ref[0], preferred_element_type=F32)
    a = jnp.square(jnp.maximum(a, 0.0)).astype(BF16)
    acc_ref[...] += jnp.dot(a, wd_ref[0], preferred_element_type=F32)

    @pl.when(j == pl.num_programs(1) - 1)
    def _():
        x = acc_ref[...]
        if split_norm_out:
            yp_ref, ys_ref = rest[0], rest[1]
            y = _rms_rows(x, g_ref[...])

            @pl.when(i < first_part_tiles)
            def _():
                yp_ref[...] = y

            @pl.when(i >= first_part_tiles)
            def _():
                ys_ref[...] = y
        else:
            xo_ref, ho_ref = rest[0], rest[1]
            xo_ref[...] = x
            ho_ref[...] = _rms_rows(x, g_ref[...]).astype(ho_ref.dtype)


MLP_HIDDEN_TILE = 1024


def _mlp(h, x, w_up, w_down, layer, norm_g, *, final_split=None):
    m, d = x.shape
    f = w_up.shape[2]
    tf = MLP_HIDDEN_TILE
    row_spec = pl.BlockSpec((ROW_TILE, d), lambda i, j: (i, 0))
    if final_split is None:
        first_part_tiles = m // ROW_TILE
        out_specs = [row_spec, row_spec]
        out_shape = [jax.ShapeDtypeStruct((m, d), F32), jax.ShapeDtypeStruct((m, d), BF16)]
    else:
        first_part_tiles = final_split // ROW_TILE
        n_sample_tiles = m // ROW_TILE - first_part_tiles
        out_specs = [
            pl.BlockSpec((ROW_TILE, d), lambda i, j: (jnp.minimum(i, first_part_tiles - 1), 0)),
            pl.BlockSpec((ROW_TILE, d), lambda i, j: (jnp.maximum(i - first_part_tiles, 0), 0))]
        out_shape = [jax.ShapeDtypeStruct((final_split, d), F32),
                     jax.ShapeDtypeStruct((n_sample_tiles * ROW_TILE, d), F32)]
    return pl.pallas_call(
        functools.partial(_mlp_kernel, split_norm_out=final_split is not None,
                          first_part_tiles=first_part_tiles),
        grid=(m // ROW_TILE, f // tf),
        in_specs=[row_spec, row_spec,
                  pl.BlockSpec((1, d, tf), lambda i, j: (layer, 0, j)),
                  pl.BlockSpec((1, tf, d), lambda i, j: (layer, j, 0)),
                  pl.BlockSpec((1, d), lambda i, j: (0, 0))],
        out_specs=out_specs,
        out_shape=out_shape,
        scratch_shapes=[pltpu.VMEM((ROW_TILE, d), F32)],
        compiler_params=_cparams("arbitrary", "arbitrary"),
        name="mlp",
    )(h, x, w_up, w_down, norm_g.reshape(1, d))


def kernel(x_prompt, x_sample, mem_prompt, cache_mem_k, cache_mem_v, state_pool, state_conv,
           norm_mix, w_in, w_pool, pool_scale, conv_w, w_out, norm_attn, norm_mem, w_q, w_k, w_v,
           w_o, norm_mlp, w_up, w_down, norm_final):
    batch, seq, d = x_prompt.shape
    nb, lq, _ = x_sample.shape
    depth = w_in.shape[0]
    pool_buf, pw = state_pool.shape[2], state_pool.shape[3]
    conv_buf, cwid = state_conv.shape[2], state_conv.shape[3]
    n_prompt_rows = batch * seq
    tiles_per_seq = seq // ROW_TILE
    assert nb * lq == ROW_TILE and pool_buf < U_HALO and conv_buf == CONV_K - 1 <= Z_HALO
    assert cache_mem_k.shape[3] == N_XHEADS

    w_up_b, w_down_b = w_up.astype(BF16), w_down.astype(BF16)

    mem_k, mem_k_heads = _mem_proj(mem_prompt, norm_mem, w_k)
    mem_v, mem_v_heads = _mem_proj(mem_prompt, norm_mem, w_v)

    x = [x_prompt.reshape(n_prompt_rows, d), x_sample.transpose(1, 0, 2).reshape(lq * nb, d)]
    h = _rms(x, norm_mix[0])
    pool_p, conv_p, pool_s, conv_s = [], [], [], []
    for l in range(depth):
        proj = _mm(h, w_in, l, tn=d)
        sp_tm = state_pool[l].transpose(1, 0, 2)
        sc_tm = state_conv[l].transpose(1, 0, 2)
        y, u_tail, z_tail, z_s = _mixer(proj, sp_tm, sc_tm, w_pool[l], pool_scale[l], conv_w[l],
                                        n_prompt_rows=n_prompt_rows, seq=seq)
        x, h = _mm(y, w_out, l, res=x, norm_g=norm_attn[l])
        last_tiles = slice(tiles_per_seq - 1, batch * tiles_per_seq, tiles_per_seq)
        pool_p.append(u_tail[last_tiles, U_HALO - pool_buf:])
        conv_p.append(z_tail[last_tiles, Z_HALO - conv_buf:])
        u_s = proj[n_prompt_rows:, :pw].reshape(lq, nb, pw).transpose(1, 0, 2)
        pool_s.append(jnp.concatenate([state_pool[l], u_s], axis=1)[:, -pool_buf:])
        conv_s.append(z_s.reshape(conv_buf, nb, cwid).transpose(1, 0, 2))
        q = _mm(h, w_q, l, out_dtype=BF16)
        o_p = _attn_prompt(q, mem_k_heads, mem_v_heads, l, batch=batch, seq=seq)
        q_s = q[n_prompt_rows:].astype(F32).reshape(lq, nb, d).transpose(1, 0, 2)
        o_s = _attn_sample(q_s, cache_mem_k, cache_mem_v, l)
        o_s = o_s.transpose(1, 0, 2).reshape(lq * nb, d).astype(BF16)
        x, h = _mm([o_p, o_s], w_o, l, res=x, norm_g=norm_mlp[l])
        if l < depth - 1:
            x, h = _mlp(h, x, w_up_b, w_down_b, l, norm_mix[l + 1])
        else:
            y_p, y_s = _mlp(h, x, w_up_b, w_down_b, l, norm_final, final_split=n_prompt_rows)

    y_prompt = y_p.reshape(batch, seq, d)
    y_sample = y_s.reshape(lq, nb, d).transpose(1, 0, 2)
    return (y_prompt, y_sample, jnp.stack(pool_p), jnp.stack(conv_p), mem_k, mem_v,
            jnp.stack(pool_s), jnp.stack(conv_s))
```

```python
import functools
import math

import jax
import jax.numpy as jnp
from jax import lax
from jax.experimental import pallas as pl
from jax.experimental.pallas import tpu as pltpu

F32 = jnp.float32
BF16 = jnp.bfloat16

PAST_LEN = 16384
POOL_WINDOWS = (2, 4, 8, 16)
CONV_K = 3
N_XHEADS = 4
EPS = 1e-6

V7X_VMEM_BYTES = 64 * 1024 * 1024
VMEM_LIMIT_BYTES = V7X_VMEM_BYTES - 8 * 1024 * 1024
SUBLANES_F32 = 8
LANES = 128

ROW_TILE = 512


def _cparams(*sem):
    return pltpu.CompilerParams(dimension_semantics=sem, vmem_limit_bytes=VMEM_LIMIT_BYTES)


def _rms_rows(x, g):
    ms = jnp.mean(x * x, axis=-1, keepdims=True)
    return x * lax.rsqrt(ms + EPS) * g


def _parts(x):
    return list(x) if isinstance(x, (list, tuple)) else [x]


def _part_specs(parts, cols, col_index, row_axis):
    specs, start = [], 0
    for p in parts:
        n = p.shape[0] // ROW_TILE
        assert p.shape[0] % ROW_TILE == 0

        def index_map(*ids, start=start, n=n):
            i = ids[row_axis]
            return (jnp.clip(i - start, 0, n - 1), col_index(*ids))

        mode = {"pipeline_mode": pl.Buffered(1)} if n == 1 and len(parts) > 1 else {}
        specs.append(pl.BlockSpec((ROW_TILE, cols), index_map, **mode))
        start += n
    return specs


def _read_parts(refs, i, first_part_tiles):
    if len(refs) == 1:
        return refs[0][...]
    return jnp.where(i < first_part_tiles, refs[0][...], refs[1][...])


def _rms_kernel(*refs, n_parts, first_part_tiles):
    x_refs, g_ref, o_ref = refs[:n_parts], refs[n_parts], refs[n_parts + 1]
    x = _read_parts(x_refs, pl.program_id(0), first_part_tiles)
    o_ref[...] = _rms_rows(x, g_ref[...]).astype(o_ref.dtype)


def _rms(x, g, out_dtype=BF16):
    parts = _parts(x)
    rows = sum(p.shape[0] for p in parts)
    d = parts[0].shape[1]
    return pl.pallas_call(
        functools.partial(_rms_kernel, n_parts=len(parts),
                          first_part_tiles=parts[0].shape[0] // ROW_TILE),
        grid=(rows // ROW_TILE,),
        in_specs=_part_specs(parts, d, lambda i: 0, 0) + [pl.BlockSpec((1, d), lambda i: (0, 0))],
        out_specs=pl.BlockSpec((ROW_TILE, d), lambda i: (i, 0)),
        out_shape=jax.ShapeDtypeStruct((rows, d), out_dtype),
        compiler_params=_cparams("parallel"),
        name="rms",
    )(*parts, g.reshape(1, d))


def _mm_kernel(*refs, kpart_sizes, n_res, has_norm, emit_out, first_part_tiles):
    refs = list(refs)
    a_refs = [[refs.pop(0) for _ in range(n)] for n in kpart_sizes]
    w_ref = refs.pop(0)
    res_refs = [refs.pop(0) for _ in range(n_res)]
    g_ref = refs.pop(0) if has_norm else None
    o_ref = refs.pop(0) if emit_out else None
    h_ref = refs.pop(0) if has_norm else None
    wb_ref = refs.pop(0)
    i = pl.program_id(0)

    @pl.when(i == 0)
    def _():
        wb_ref[...] = w_ref[0].astype(BF16)

    acc, k0 = None, 0
    for part_refs in a_refs:
        a = _read_parts(part_refs, i, first_part_tiles)
        t = jnp.dot(a, wb_ref[k0:k0 + a.shape[1], :], preferred_element_type=F32)
        acc = t if acc is None else acc + t
        k0 += a.shape[1]
    if n_res:
        acc = acc + _read_parts(res_refs, i, first_part_tiles)
    if emit_out:
        o_ref[...] = acc.astype(o_ref.dtype)
    if has_norm:
        h_ref[...] = _rms_rows(acc, g_ref[...]).astype(h_ref.dtype)


def _mm(a_kparts, w, layer, *, res=None, norm_g=None, out_dtype=F32, norm_dtype=BF16,
        emit_out=True):
    a_kparts = [_parts(p) for p in a_kparts]
    res_parts = _parts(res) if res is not None else []
    m = sum(p.shape[0] for p in a_kparts[0])
    _, k, n = w.shape
    assert sum(ps[0].shape[1] for ps in a_kparts) == k
    split = {ps[0].shape[0] // ROW_TILE for ps in a_kparts + [res_parts] if len(ps) == 2}
    assert len(split) <= 1
    first_part_tiles = split.pop() if split else m // ROW_TILE
    in_specs, args = [], []
    for ps in a_kparts:
        in_specs += _part_specs(ps, ps[0].shape[1], lambda i: 0, 0)
        args += ps
    in_specs.append(pl.BlockSpec((1, k, n), lambda i: (layer, 0, 0), pipeline_mode=pl.Buffered(1)))
    in_specs += _part_specs(res_parts, n, lambda i: 0, 0)
    args += [w] + res_parts
    if norm_g is not None:
        in_specs.append(pl.BlockSpec((1, n), lambda i: (0, 0)))
        args.append(norm_g.reshape(1, n))
    out_specs, out_shape = [], []
    if emit_out:
        out_specs.append(pl.BlockSpec((ROW_TILE, n), lambda i: (i, 0)))
        out_shape.append(jax.ShapeDtypeStruct((m, n), out_dtype))
    if norm_g is not None:
        out_specs.append(pl.BlockSpec((ROW_TILE, n), lambda i: (i, 0)))
        out_shape.append(jax.ShapeDtypeStruct((m, n), norm_dtype))
    outs = pl.pallas_call(
        functools.partial(_mm_kernel, kpart_sizes=tuple(len(ps) for ps in a_kparts),
                          n_res=len(res_parts), has_norm=norm_g is not None, emit_out=emit_out,
                          first_part_tiles=first_part_tiles),
        grid=(m // ROW_TILE,),
        in_specs=in_specs,
        out_specs=out_specs,
        out_shape=out_shape,
        scratch_shapes=[pltpu.VMEM((k, n), BF16)],
        compiler_params=_cparams("arbitrary"),
        name="mm",
    )(*args)
    return outs[0] if len(outs) == 1 else tuple(outs)


MEM_ROW_TILE = 256


def _mem_proj_kernel(m_ref, g_ref, w_ref, o_ref, ob_ref, wb_ref):
    @pl.when(pl.program_id(1) == 0)
    def _():
        wb_ref[...] = w_ref[0].astype(BF16)

    h = _rms_rows(m_ref[...], g_ref[0]).astype(BF16)
    kv = jnp.dot(h, wb_ref[...], preferred_element_type=F32)
    hd = o_ref.shape[-1]
    for head in range(N_XHEADS):
        o_ref[0, 0, :, head, :] = kv[:, head * hd:(head + 1) * hd]
        ob_ref[0, 0, head] = kv[:, head * hd:(head + 1) * hd].astype(BF16)


def _mem_proj(mem, norm_mem, w):
    batch, n_mem, d = mem.shape
    depth = w.shape[0]
    tm = MEM_ROW_TILE
    assert n_mem % tm == 0
    tiles_per_seq = n_mem // tm
    hd = d // N_XHEADS
    return pl.pallas_call(
        _mem_proj_kernel,
        grid=(depth, batch * tiles_per_seq),
        in_specs=[pl.BlockSpec((tm, d), lambda l, i: (i, 0)),
                  pl.BlockSpec((1, 1, d), lambda l, i: (l, 0, 0)),
                  pl.BlockSpec((1, d, d), lambda l, i: (l, 0, 0))],
        out_specs=[pl.BlockSpec((1, 1, tm, N_XHEADS, hd),
                                lambda l, i: (l, i // tiles_per_seq, i % tiles_per_seq, 0, 0)),
                   pl.BlockSpec((1, 1, N_XHEADS, tm, hd),
                                lambda l, i: (l, i // tiles_per_seq, 0, i % tiles_per_seq, 0))],
        out_shape=[jax.ShapeDtypeStruct((depth, batch, n_mem, N_XHEADS, hd), F32),
                   jax.ShapeDtypeStruct((depth, batch, N_XHEADS, n_mem, hd), BF16)],
        scratch_shapes=[pltpu.VMEM((d, d), BF16)],
        compiler_params=_cparams("arbitrary", "arbitrary"),
        name="mem_proj",
    )(mem.reshape(batch * n_mem, d), norm_mem.reshape(depth, 1, d), w)


U_HALO = 16
Z_HALO = 8
CAST_CHUNK = LANES


def _mixin_kernel(h_ref, wu_ref, wb_ref, wc_ref, wv_ref, sp_ref, sc_ref, wp_ref, ps_ref, cw_ref,
                  up_ref, dn_ref,
                  ya_ref, yb_ref, ut_ref, zt_ref, us_ref, zs_ref, upb_ref, dnb_ref,
                  wcat_ref, hu_ref, hz_ref,
                  *, n_prompt_tiles, tiles_per_seq, dec_batch, dec_seq):
    tm = ROW_TILE
    g, i = pl.program_id(0), pl.program_id(1)
    gc = hu_ref.shape[1]
    cw0, cw1, cw2 = cw_ref[0:1, :], cw_ref[1:2, :], cw_ref[2:3, :]

    def for_group(values):
        out = values[-1]
        for gg in range(len(values) - 2, -1, -1):
            out = jnp.where(g == gg, values[gg], out)
        return out

    def project():
        upb_ref[...] = up_ref[0].astype(BF16)
        dnb_ref[...] = dn_ref[0].astype(BF16)
        pj = jnp.dot(h_ref[...], wcat_ref[...], preferred_element_type=F32)
        return pj[:, 0:gc], pj[:, gc:2 * gc], pj[:, 2 * gc:3 * gc], pj[:, 3 * gc:4 * gc]

    def finish(d, yb):
        ya = jnp.dot(d.astype(BF16), wp_ref[0, 0].astype(BF16), preferred_element_type=F32)
        ya_ref[...] = (ya * ps_ref[...]).astype(BF16)
        yb_ref[...] = yb.astype(BF16)

    @pl.when(i == 0)
    def _():
        for k, w_ref in enumerate((wu_ref, wb_ref, wc_ref, wv_ref)):
            wcat_ref[:, k * gc:(k + 1) * gc] = w_ref[0].astype(BF16)

    @pl.when(i < n_prompt_tiles)
    def _prompt():
        start = (i % tiles_per_seq) * tm

        @pl.when(start == 0)
        def _():
            hu_ref[...] = jnp.zeros(hu_ref.shape, F32)
            hz_ref[...] = jnp.zeros(hz_ref.shape, F32)

        u, bg, cg, v = project()
        ext = jnp.concatenate([hu_ref[...], u], axis=0)
        sums, width = [], 1
        for w in POOL_WINDOWS:
            while width < w:
                ext = ext + pltpu.roll(ext, width, 0)
                width *= 2
            sums.append(ext)
        s = for_group(sums)[U_HALO:]
        w_here = for_group([jnp.int32(w) for w in POOL_WINDOWS])
        inv_w = for_group([jnp.float32(1.0 / w) for w in POOL_WINDOWS])
        pos = start + lax.broadcasted_iota(jnp.int32, (U_HALO, 1), 0)
        count = jnp.minimum(w_here, pos + 1).astype(F32)
        mean = jnp.concatenate([s[:U_HALO] / count, s[U_HALO:] * inv_w], axis=0)
        z = cg * v
        zext = jnp.concatenate([hz_ref[...], z], axis=0)
        c = (cw0 * pltpu.roll(zext, 2, 0)[Z_HALO:] + cw1 * pltpu.roll(zext, 1, 0)[Z_HALO:]
             + cw2 * z)
        finish(mean - u, bg * c)
        ut_ref[0] = u[tm - U_HALO:]
        zt_ref[0] = z[tm - Z_HALO:]
        hu_ref[...] = u[tm - U_HALO:]
        hz_ref[...] = z[tm - Z_HALO:]

    @pl.when(i >= n_prompt_tiles)
    def _sample():
        nb = dec_batch
        n_prev = sp_ref.shape[0]
        u, bg, cg, v = project()
        us_ref[...] = u
        ext = [sp_ref[t] for t in range(n_prev)] + [u[l * nb:(l + 1) * nb] for l in range(dec_seq)]
        d = []
        for l in range(dec_seq):
            t = n_prev + l
            run, sums = ext[t], []
            for j in range(1, max(POOL_WINDOWS)):
                if j in POOL_WINDOWS:
                    sums.append(run)
                run = run + ext[t - j]
            sums.append(run)
            inv_count = for_group([jnp.float32(1.0 / min(w, PAST_LEN + l + 1))
                                   for w in POOL_WINDOWS])
            d.append(for_group(sums) * inv_count - ext[t])
        z = cg * v
        zext = [sc_ref[t] for t in range(sc_ref.shape[0])] + [z[l * nb:(l + 1) * nb]
                                                              for l in range(dec_seq)]
        c = [cw0 * zext[l] + cw1 * zext[l + 1] + cw2 * zext[l + 2] for l in range(dec_seq)]
        finish(jnp.concatenate(d, axis=0), bg * jnp.concatenate(c, axis=0))
        zs_ref[...] = z[tm - zs_ref.shape[0]:]
        ut_ref[0] = u[tm - U_HALO:]
        zt_ref[0] = jnp.zeros(zt_ref.shape[1:], F32)


def _mixin(h, w_in, layer, state_pool_tm, state_conv_tm, w_pool, pool_scale, conv_w,
           w_up, w_down, *, n_prompt_rows, seq):
    rows, d = h.shape
    tm = ROW_TILE
    n_groups = len(POOL_WINDOWS)
    pw = state_pool_tm.shape[2]
    cwid = state_conv_tm.shape[2]
    gc = pw // n_groups
    assert w_in.shape[2] == pw + 3 * cwid and cwid == pw
    n_tiles = rows // tm
    n_prompt_tiles = n_prompt_rows // tm
    dec_batch = state_pool_tm.shape[1]
    dec_seq = (rows - n_prompt_rows) // dec_batch
    assert n_tiles == n_prompt_tiles + 1 and dec_seq * dec_batch == tm and seq % tm == 0
    n_conv_prev = state_conv_tm.shape[0]
    f = w_up.shape[2]
    n_chunks = f // CAST_CHUNK
    assert n_chunks <= n_groups * n_tiles and f % CAST_CHUNK == 0

    def chunk(g, i):
        return jnp.minimum(g * n_tiles + i, n_chunks - 1)

    def w_in_spec(k):
        return pl.BlockSpec((1, d, gc), lambda g, i: (layer, 0, k * n_groups + g))

    kernel = functools.partial(_mixin_kernel, n_prompt_tiles=n_prompt_tiles,
                               tiles_per_seq=seq // tm, dec_batch=dec_batch, dec_seq=dec_seq)
    return pl.pallas_call(
        kernel,
        grid=(n_groups, n_tiles),
        in_specs=[
            pl.BlockSpec((tm, d), lambda g, i: (i, 0)),
            w_in_spec(0), w_in_spec(1), w_in_spec(2), w_in_spec(3),
            pl.BlockSpec(state_pool_tm.shape[:2] + (gc,), lambda g, i: (0, 0, g)),
            pl.BlockSpec(state_conv_tm.shape[:2] + (gc,), lambda g, i: (0, 0, g)),
            pl.BlockSpec((1, 1, gc, gc), lambda g, i: (layer, g, 0, 0)),
            pl.BlockSpec((1, gc), lambda g, i: (0, g)),
            pl.BlockSpec((conv_w.shape[0], gc), lambda g, i: (0, g)),
            pl.BlockSpec((1, d, CAST_CHUNK), lambda g, i: (layer, 0, chunk(g, i))),
            pl.BlockSpec((1, CAST_CHUNK, d), lambda g, i: (layer, chunk(g, i), 0)),
        ],
        out_specs=[
            pl.BlockSpec((tm, gc), lambda g, i: (i, g)),
            pl.BlockSpec((tm, gc), lambda g, i: (i, g)),
            pl.BlockSpec((1, U_HALO, gc), lambda g, i: (i, 0, g)),
            pl.BlockSpec((1, Z_HALO, gc), lambda g, i: (i, 0, g)),
            pl.BlockSpec((tm, gc), lambda g, i: (0, g)),
            pl.BlockSpec((n_conv_prev * dec_batch, gc), lambda g, i: (0, g)),
            pl.BlockSpec((d, CAST_CHUNK), lambda g, i: (0, chunk(g, i))),
            pl.BlockSpec((CAST_CHUNK, d), lambda g, i: (chunk(g, i), 0)),
        ],
        out_shape=[
            jax.ShapeDtypeStruct((rows, pw), BF16),
            jax.ShapeDtypeStruct((rows, cwid), BF16),
            jax.ShapeDtypeStruct((n_tiles, U_HALO, pw), F32),
            jax.ShapeDtypeStruct((n_tiles, Z_HALO, cwid), F32),
            jax.ShapeDtypeStruct((tm, pw), F32),
            jax.ShapeDtypeStruct((n_conv_prev * dec_batch, cwid), F32),
            jax.ShapeDtypeStruct((d, f), BF16),
            jax.ShapeDtypeStruct((f, d), BF16),
        ],
        scratch_shapes=[
            pltpu.VMEM((d, 4 * gc), BF16),
            pltpu.VMEM((U_HALO, gc), F32),
            pltpu.VMEM((Z_HALO, gc), F32),
        ],
        compiler_params=_cparams("arbitrary", "arbitrary"),
        name="mixin",
    )(h, w_in, w_in, w_in, w_in, state_pool_tm, state_conv_tm, w_pool,
      pool_scale.reshape(1, pw), conv_w, w_up, w_down)


def _masked_softmax_rows(s, valid):
    s = jnp.where(valid, s, -jnp.inf)
    m = jnp.max(s, axis=-1, keepdims=True)
    e = jnp.exp(s - m)
    return e * (1.0 / jnp.sum(e, axis=-1, keepdims=True))


def _softmax_rows(s):
    m = jnp.max(s, axis=-1, keepdims=True)
    e = jnp.exp(s - m)
    return e * (1.0 / jnp.sum(e, axis=-1, keepdims=True))


def _attn_prompt_kernel(q_ref, k_ref, v_ref, o_ref):
    hd = k_ref.shape[-1]
    scale = 1.0 / math.sqrt(hd)
    for h in range(N_XHEADS):
        cols = slice(h * hd, (h + 1) * hd)
        s = lax.dot_general(q_ref[:, cols], k_ref[0, 0, h], (((1,), (1,)), ((), ())),
                            preferred_element_type=F32) * scale
        p = _softmax_rows(s).astype(BF16)
        o_ref[:, cols] = jnp.dot(p, v_ref[0, 0, h], preferred_element_type=F32).astype(o_ref.dtype)


def _attn_prompt(q, mk, mv, layer, *, batch, seq):
    d = q.shape[1]
    kv_block = (1, 1) + mk.shape[2:]
    tq = ROW_TILE
    tps = seq // tq
    return pl.pallas_call(
        _attn_prompt_kernel,
        grid=(batch, tps),
        in_specs=[pl.BlockSpec((tq, d), lambda b, t: (b * tps + t, 0)),
                  pl.BlockSpec(kv_block, lambda b, t: (layer, b, 0, 0, 0)),
                  pl.BlockSpec(kv_block, lambda b, t: (layer, b, 0, 0, 0))],
        out_specs=pl.BlockSpec((tq, d), lambda b, t: (b * tps + t, 0)),
        out_shape=jax.ShapeDtypeStruct((batch * seq, d), BF16),
        compiler_params=_cparams("parallel", "parallel"),
        name="attn_prompt",
    )(q, mk, mv)


SAMPLE_ATTN_BATCH_BLOCK = 4


def _attn_sample_kernel(q_ref, k_ref, v_ref, o_ref):
    bb, lq, d = q_ref.shape
    n_mem, n_heads, hd = k_ref.shape[2:]
    scale = 1.0 / math.sqrt(hd)
    n_rows = n_heads * SUBLANES_F32
    n_keys = n_mem * n_heads
    head_of_row = lax.broadcasted_iota(jnp.int32, (n_rows, n_keys), 0) // SUBLANES_F32
    head_of_key = lax.broadcasted_iota(jnp.int32, (n_rows, n_keys), 1) % n_heads
    valid = head_of_row == head_of_key
    for i in range(bb):
        q8 = jnp.concatenate([q_ref[i], jnp.zeros((SUBLANES_F32 - lq, d), F32)], axis=0)
        qh = jnp.concatenate([q8[:, h * hd:(h + 1) * hd] for h in range(n_heads)], axis=0)
        k2 = k_ref[0, i].reshape(n_keys, hd).astype(BF16)
        v2 = v_ref[0, i].reshape(n_keys, hd).astype(BF16)
        s = lax.dot_general(qh.astype(BF16), k2, (((1,), (1,)), ((), ())),
                            preferred_element_type=F32) * scale
        p = _masked_softmax_rows(s, valid).astype(BF16)
        o = jnp.dot(p, v2, preferred_element_type=F32)
        o_ref[i] = jnp.concatenate(
            [o[h * SUBLANES_F32:h * SUBLANES_F32 + lq] for h in range(n_heads)], axis=1)


def _attn_sample(q, cache_k, cache_v, layer):
    nb, lq, d = q.shape
    bb = SAMPLE_ATTN_BATCH_BLOCK
    kv_block = (1, bb) + cache_k.shape[2:]
    assert lq <= SUBLANES_F32 and nb % bb == 0
    return pl.pallas_call(
        _attn_sample_kernel,
        grid=(nb // bb,),
        in_specs=[pl.BlockSpec((bb, lq, d), lambda j: (j, 0, 0)),
                  pl.BlockSpec(kv_block, lambda j: (layer, j, 0, 0, 0)),
                  pl.BlockSpec(kv_block, lambda j: (layer, j, 0, 0, 0))],
        out_specs=pl.BlockSpec((bb, lq, d), lambda j: (j, 0, 0)),
        out_shape=jax.ShapeDtypeStruct((nb, lq, d), F32),
        compiler_params=_cparams("parallel"),
        name="attn_sample",
    )(q, cache_k, cache_v)


def _mlp_kernel(h_ref, x_ref, wu_ref, wd_ref, g_ref, *rest, split_norm_out, first_part_tiles):
    acc_ref = rest[-1]
    i, j = pl.program_id(0), pl.program_id(1)

    @pl.when(j == 0)
    def _():
        acc_ref[...] = x_ref[...]

    a = jnp.dot(h_ref[...], wu_ref[...], preferred_element_type=F32)
    a = jnp.square(jnp.maximum(a, 0.0)).astype(BF16)
    acc_ref[...] += jnp.dot(a, wd_ref[...], preferred_element_type=F32)

    @pl.when(j == pl.num_programs(1) - 1)
    def _():
        x = acc_ref[...]
        if split_norm_out:
            yp_ref, ys_ref = rest[0], rest[1]
            y = _rms_rows(x, g_ref[...])

            @pl.when(i < first_part_tiles)
            def _():
                yp_ref[...] = y

            @pl.when(i >= first_part_tiles)
            def _():
                ys_ref[...] = y
        else:
            xo_ref, ho_ref = rest[0], rest[1]
            xo_ref[...] = x
            ho_ref[...] = _rms_rows(x, g_ref[...]).astype(ho_ref.dtype)


MLP_HIDDEN_TILE = 1024


def _mlp(h, x, w_up, w_down, norm_g, *, final_split=None):
    m, d = x.shape
    f = w_up.shape[1]
    tf = MLP_HIDDEN_TILE
    row_spec = pl.BlockSpec((ROW_TILE, d), lambda i, j: (i, 0))
    if final_split is None:
        first_part_tiles = m // ROW_TILE
        out_specs = [row_spec, row_spec]
        out_shape = [jax.ShapeDtypeStruct((m, d), F32), jax.ShapeDtypeStruct((m, d), BF16)]
    else:
        first_part_tiles = final_split // ROW_TILE
        n_sample_tiles = m // ROW_TILE - first_part_tiles
        out_specs = [
            pl.BlockSpec((ROW_TILE, d), lambda i, j: (jnp.minimum(i, first_part_tiles - 1), 0)),
            pl.BlockSpec((ROW_TILE, d), lambda i, j: (jnp.maximum(i - first_part_tiles, 0), 0))]
        out_shape = [jax.ShapeDtypeStruct((final_split, d), F32),
                     jax.ShapeDtypeStruct((n_sample_tiles * ROW_TILE, d), F32)]
    return pl.pallas_call(
        functools.partial(_mlp_kernel, split_norm_out=final_split is not None,
                          first_part_tiles=first_part_tiles),
        grid=(m // ROW_TILE, f // tf),
        in_specs=[row_spec, row_spec,
                  pl.BlockSpec((d, tf), lambda i, j: (0, j)),
                  pl.BlockSpec((tf, d), lambda i, j: (j, 0)),
                  pl.BlockSpec((1, d), lambda i, j: (0, 0))],
        out_specs=out_specs,
        out_shape=out_shape,
        scratch_shapes=[pltpu.VMEM((ROW_TILE, d), F32)],
        compiler_params=_cparams("arbitrary", "arbitrary"),
        name="mlp",
    )(h, x, w_up, w_down, norm_g.reshape(1, d))


def kernel(x_prompt, x_sample, mem_prompt, cache_mem_k, cache_mem_v, state_pool, state_conv,
           norm_mix, w_in, w_pool, pool_scale, conv_w, w_out, norm_attn, norm_mem, w_q, w_k, w_v,
           w_o, norm_mlp, w_up, w_down, norm_final):
    batch, seq, d = x_prompt.shape
    nb, lq, _ = x_sample.shape
    depth = w_in.shape[0]
    pool_buf, pw = state_pool.shape[2], state_pool.shape[3]
    conv_buf, cwid = state_conv.shape[2], state_conv.shape[3]
    n_prompt_rows = batch * seq
    tiles_per_seq = seq // ROW_TILE
    assert nb * lq == ROW_TILE and pool_buf < U_HALO and conv_buf == CONV_K - 1 <= Z_HALO
    assert cache_mem_k.shape[3] == N_XHEADS

    mem_k, mem_k_heads = _mem_proj(mem_prompt, norm_mem, w_k)
    mem_v, mem_v_heads = _mem_proj(mem_prompt, norm_mem, w_v)

    x = [x_prompt.reshape(n_prompt_rows, d), x_sample.transpose(1, 0, 2).reshape(lq * nb, d)]
    h = _rms(x, norm_mix[0])
    pool_p, conv_p, pool_s, conv_s = [], [], [], []
    for l in range(depth):
        sp_tm = state_pool[l].transpose(1, 0, 2)
        sc_tm = state_conv[l].transpose(1, 0, 2)
        ya, yb, u_tail, z_tail, u_s, z_s, w_up_b, w_down_b = _mixin(
            h, w_in, l, sp_tm, sc_tm, w_pool, pool_scale[l], conv_w[l], w_up, w_down,
            n_prompt_rows=n_prompt_rows, seq=seq)
        x, h = _mm([ya, yb], w_out, l, res=x, norm_g=norm_attn[l])
        last_tiles = slice(tiles_per_seq - 1, batch * tiles_per_seq, tiles_per_seq)
        pool_p.append(u_tail[last_tiles, U_HALO - pool_buf:])
        conv_p.append(z_tail[last_tiles, Z_HALO - conv_buf:])
        u_s = u_s.reshape(lq, nb, pw).transpose(1, 0, 2)
        pool_s.append(jnp.concatenate([state_pool[l], u_s], axis=1)[:, -pool_buf:])
        conv_s.append(z_s.reshape(conv_buf, nb, cwid).transpose(1, 0, 2))
        q = _mm([h], w_q, l, out_dtype=BF16)
        o_p = _attn_prompt(q, mem_k_heads, mem_v_heads, l, batch=batch, seq=seq)
        q_s = q[n_prompt_rows:].astype(F32).reshape(lq, nb, d).transpose(1, 0, 2)
        o_s = _attn_sample(q_s, cache_mem_k, cache_mem_v, l)
        o_s = o_s.transpose(1, 0, 2).reshape(lq * nb, d).astype(BF16)
        x, h = _mm([[o_p, o_s]], w_o, l, res=x, norm_g=norm_mlp[l])
        if l < depth - 1:
            x, h = _mlp(h, x, w_up_b, w_down_b, norm_mix[l + 1])
        else:
            y_p, y_s = _mlp(h, x, w_up_b, w_down_b, norm_final, final_split=n_prompt_rows)

    y_prompt = y_p.reshape(batch, seq, d)
    y_sample = y_s.reshape(lq, nb, d).transpose(1, 0, 2)
    return (y_prompt, y_sample, jnp.stack(pool_p), jnp.stack(conv_p), mem_k, mem_v,
            jnp.stack(pool_s), jnp.stack(conv_s))
```

```python
import functools
import math

import jax
import jax.numpy as jnp
from jax import lax
from jax.experimental import pallas as pl
from jax.experimental.pallas import tpu as pltpu

F32 = jnp.float32
BF16 = jnp.bfloat16

PAST_LEN = 16384
POOL_WINDOWS = (2, 4, 8, 16)
CONV_K = 3
N_XHEADS = 4
EPS = 1e-6

V7X_VMEM_BYTES = 64 * 1024 * 1024
VMEM_LIMIT_BYTES = V7X_VMEM_BYTES - 8 * 1024 * 1024
SUBLANES_F32 = 8
LANES = 128

ROW_TILE = 512


def _cparams(*sem):
    return pltpu.CompilerParams(dimension_semantics=sem, vmem_limit_bytes=VMEM_LIMIT_BYTES)


def _rms_rows(x, g):
    ms = jnp.mean(x * x, axis=-1, keepdims=True)
    return x * lax.rsqrt(ms + EPS) * g


def _parts(x):
    return list(x) if isinstance(x, (list, tuple)) else [x]


def _part_specs(parts, cols, col_index, row_axis):
    specs, start = [], 0
    for p in parts:
        n = p.shape[0] // ROW_TILE
        assert p.shape[0] % ROW_TILE == 0

        def index_map(*ids, start=start, n=n):
            i = ids[row_axis]
            return (jnp.clip(i - start, 0, n - 1), col_index(*ids))

        mode = {"pipeline_mode": pl.Buffered(1)} if n == 1 and len(parts) > 1 else {}
        specs.append(pl.BlockSpec((ROW_TILE, cols), index_map, **mode))
        start += n
    return specs


def _read_parts(refs, i, first_part_tiles):
    if len(refs) == 1:
        return refs[0][...]
    return jnp.where(i < first_part_tiles, refs[0][...], refs[1][...])


def _rms_kernel(*refs, n_parts, first_part_tiles):
    x_refs, g_ref, o_ref = refs[:n_parts], refs[n_parts], refs[n_parts + 1]
    x = _read_parts(x_refs, pl.program_id(0), first_part_tiles)
    o_ref[...] = _rms_rows(x, g_ref[...]).astype(o_ref.dtype)


def _rms(x, g, out_dtype=BF16):
    parts = _parts(x)
    rows = sum(p.shape[0] for p in parts)
    d = parts[0].shape[1]
    return pl.pallas_call(
        functools.partial(_rms_kernel, n_parts=len(parts),
                          first_part_tiles=parts[0].shape[0] // ROW_TILE),
        grid=(rows // ROW_TILE,),
        in_specs=_part_specs(parts, d, lambda i: 0, 0) + [pl.BlockSpec((1, d), lambda i: (0, 0))],
        out_specs=pl.BlockSpec((ROW_TILE, d), lambda i: (i, 0)),
        out_shape=jax.ShapeDtypeStruct((rows, d), out_dtype),
        compiler_params=_cparams("parallel"),
        name="rms",
    )(*parts, g.reshape(1, d))


def _mm_kernel(*refs, kpart_sizes, n_res, has_norm, emit_out, first_part_tiles):
    refs = list(refs)
    a_refs = [[refs.pop(0) for _ in range(n)] for n in kpart_sizes]
    w_ref = refs.pop(0)
    res_refs = [refs.pop(0) for _ in range(n_res)]
    g_ref = refs.pop(0) if has_norm else None
    o_ref = refs.pop(0) if emit_out else None
    h_ref = refs.pop(0) if has_norm else None
    wb_ref = refs.pop(0)
    i = pl.program_id(0)

    @pl.when(i == 0)
    def _():
        wb_ref[...] = w_ref[0].astype(BF16)

    acc, k0 = None, 0
    for part_refs in a_refs:
        a = _read_parts(part_refs, i, first_part_tiles)
        t = jnp.dot(a, wb_ref[k0:k0 + a.shape[1], :], preferred_element_type=F32)
        acc = t if acc is None else acc + t
        k0 += a.shape[1]
    if n_res:
        acc = acc + _read_parts(res_refs, i, first_part_tiles)
    if emit_out:
        o_ref[...] = acc.astype(o_ref.dtype)
    if has_norm:
        h_ref[...] = _rms_rows(acc, g_ref[...]).astype(h_ref.dtype)


def _mm(a_kparts, w, layer, *, res=None, norm_g=None, out_dtype=F32, norm_dtype=BF16,
        emit_out=True):
    a_kparts = [_parts(p) for p in a_kparts]
    res_parts = _parts(res) if res is not None else []
    m = sum(p.shape[0] for p in a_kparts[0])
    _, k, n = w.shape
    assert sum(ps[0].shape[1] for ps in a_kparts) == k
    split = {ps[0].shape[0] // ROW_TILE for ps in a_kparts + [res_parts] if len(ps) == 2}
    assert len(split) <= 1
    first_part_tiles = split.pop() if split else m // ROW_TILE
    in_specs, args = [], []
    for ps in a_kparts:
        in_specs += _part_specs(ps, ps[0].shape[1], lambda i: 0, 0)
        args += ps
    in_specs.append(pl.BlockSpec((1, k, n), lambda i: (layer, 0, 0), pipeline_mode=pl.Buffered(1)))
    in_specs += _part_specs(res_parts, n, lambda i: 0, 0)
    args += [w] + res_parts
    if norm_g is not None:
        in_specs.append(pl.BlockSpec((1, n), lambda i: (0, 0)))
        args.append(norm_g.reshape(1, n))
    out_specs, out_shape = [], []
    if emit_out:
        out_specs.append(pl.BlockSpec((ROW_TILE, n), lambda i: (i, 0)))
        out_shape.append(jax.ShapeDtypeStruct((m, n), out_dtype))
    if norm_g is not None:
        out_specs.append(pl.BlockSpec((ROW_TILE, n), lambda i: (i, 0)))
        out_shape.append(jax.ShapeDtypeStruct((m, n), norm_dtype))
    outs = pl.pallas_call(
        functools.partial(_mm_kernel, kpart_sizes=tuple(len(ps) for ps in a_kparts),
                          n_res=len(res_parts), has_norm=norm_g is not None, emit_out=emit_out,
                          first_part_tiles=first_part_tiles),
        grid=(m // ROW_TILE,),
        in_specs=in_specs,
        out_specs=out_specs,
        out_shape=out_shape,
        scratch_shapes=[pltpu.VMEM((k, n), BF16)],
        compiler_params=_cparams("arbitrary"),
        name="mm",
    )(*args)
    return outs[0] if len(outs) == 1 else tuple(outs)


MEM_ROW_TILE = 256


def _mem_proj_kernel(m_ref, g_ref, w_ref, o_ref, ob_ref, wb_ref):
    @pl.when(pl.program_id(1) == 0)
    def _():
        wb_ref[...] = w_ref[0].astype(BF16)

    h = _rms_rows(m_ref[...], g_ref[0]).astype(BF16)
    kv = jnp.dot(h, wb_ref[...], preferred_element_type=F32)
    hd = o_ref.shape[-1]
    for head in range(N_XHEADS):
        o_ref[0, 0, :, head, :] = kv[:, head * hd:(head + 1) * hd]
        ob_ref[0, 0, head] = kv[:, head * hd:(head + 1) * hd].astype(BF16)


def _mem_proj(mem, norm_mem, w):
    batch, n_mem, d = mem.shape
    depth = w.shape[0]
    tm = MEM_ROW_TILE
    assert n_mem % tm == 0
    tiles_per_seq = n_mem // tm
    hd = d // N_XHEADS
    return pl.pallas_call(
        _mem_proj_kernel,
        grid=(depth, batch * tiles_per_seq),
        in_specs=[pl.BlockSpec((tm, d), lambda l, i: (i, 0)),
                  pl.BlockSpec((1, 1, d), lambda l, i: (l, 0, 0)),
                  pl.BlockSpec((1, d, d), lambda l, i: (l, 0, 0))],
        out_specs=[pl.BlockSpec((1, 1, tm, N_XHEADS, hd),
                                lambda l, i: (l, i // tiles_per_seq, i % tiles_per_seq, 0, 0)),
                   pl.BlockSpec((1, 1, N_XHEADS, tm, hd),
                                lambda l, i: (l, i // tiles_per_seq, 0, i % tiles_per_seq, 0))],
        out_shape=[jax.ShapeDtypeStruct((depth, batch, n_mem, N_XHEADS, hd), F32),
                   jax.ShapeDtypeStruct((depth, batch, N_XHEADS, n_mem, hd), BF16)],
        scratch_shapes=[pltpu.VMEM((d, d), BF16)],
        compiler_params=_cparams("arbitrary", "arbitrary"),
        name="mem_proj",
    )(mem.reshape(batch * n_mem, d), norm_mem.reshape(depth, 1, d), w)


U_HALO = 16
Z_HALO = 8
CAST_CHUNK = LANES


def _mixin_kernel(*refs, n_h_parts, n_prompt_tiles, tiles_per_seq, dec_batch, dec_seq):
    h_refs, refs = refs[:n_h_parts], refs[n_h_parts:]
    (wu_ref, wb_ref, wc_ref, wv_ref, sp_ref, sc_ref, wp_ref, ps_ref, cw_ref, up_ref, dn_ref,
     ya_ref, yb_ref, ut_ref, zt_ref, us_ref, zs_ref, upb_ref, dnb_ref,
     wcat_ref, hu_ref, hz_ref) = refs
    tm = ROW_TILE
    g, i = pl.program_id(0), pl.program_id(1)
    gc = hu_ref.shape[1]
    cw0, cw1, cw2 = cw_ref[0:1, :], cw_ref[1:2, :], cw_ref[2:3, :]

    def for_group(values):
        out = values[-1]
        for gg in range(len(values) - 2, -1, -1):
            out = jnp.where(g == gg, values[gg], out)
        return out

    def project(h_ref):
        upb_ref[...] = up_ref[0].astype(BF16)
        dnb_ref[...] = dn_ref[0].astype(BF16)
        pj = jnp.dot(h_ref[...], wcat_ref[...], preferred_element_type=F32)
        return pj[:, 0:gc], pj[:, gc:2 * gc], pj[:, 2 * gc:3 * gc], pj[:, 3 * gc:4 * gc]

    def finish(d, yb):
        ya = jnp.dot(d.astype(BF16), wp_ref[0, 0].astype(BF16), preferred_element_type=F32)
        ya_ref[...] = (ya * ps_ref[...]).astype(BF16)
        yb_ref[...] = yb.astype(BF16)

    @pl.when(i == 0)
    def _():
        for k, w_ref in enumerate((wu_ref, wb_ref, wc_ref, wv_ref)):
            wcat_ref[:, k * gc:(k + 1) * gc] = w_ref[0].astype(BF16)

    @pl.when(i < n_prompt_tiles)
    def _prompt():
        start = (i % tiles_per_seq) * tm

        @pl.when(start == 0)
        def _():
            hu_ref[...] = jnp.zeros(hu_ref.shape, F32)
            hz_ref[...] = jnp.zeros(hz_ref.shape, F32)

        u, bg, cg, v = project(h_refs[0])
        ext = jnp.concatenate([hu_ref[...], u], axis=0)
        sums, width = [], 1
        for w in POOL_WINDOWS:
            while width < w:
                ext = ext + pltpu.roll(ext, width, 0)
                width *= 2
            sums.append(ext)
        s = for_group(sums)[U_HALO:]
        w_here = for_group([jnp.int32(w) for w in POOL_WINDOWS])
        inv_w = for_group([jnp.float32(1.0 / w) for w in POOL_WINDOWS])
        pos = start + lax.broadcasted_iota(jnp.int32, (U_HALO, 1), 0)
        count = jnp.minimum(w_here, pos + 1).astype(F32)
        mean = jnp.concatenate([s[:U_HALO] / count, s[U_HALO:] * inv_w], axis=0)
        z = cg * v
        zext = jnp.concatenate([hz_ref[...], z], axis=0)
        c = (cw0 * pltpu.roll(zext, 2, 0)[Z_HALO:] + cw1 * pltpu.roll(zext, 1, 0)[Z_HALO:]
             + cw2 * z)
        finish(mean - u, bg * c)
        ut_ref[0] = u[tm - U_HALO:]
        zt_ref[0] = z[tm - Z_HALO:]
        hu_ref[...] = u[tm - U_HALO:]
        hz_ref[...] = z[tm - Z_HALO:]

    @pl.when(i >= n_prompt_tiles)
    def _sample():
        nb = dec_batch
        n_prev = sp_ref.shape[0]
        u, bg, cg, v = project(h_refs[-1])
        us_ref[...] = u
        ext = [sp_ref[t] for t in range(n_prev)] + [u[l * nb:(l + 1) * nb] for l in range(dec_seq)]
        d = []
        for l in range(dec_seq):
            t = n_prev + l
            run, sums = ext[t], []
            for j in range(1, max(POOL_WINDOWS)):
                if j in POOL_WINDOWS:
                    sums.append(run)
                run = run + ext[t - j]
            sums.append(run)
            inv_count = for_group([jnp.float32(1.0 / min(w, PAST_LEN + l + 1))
                                   for w in POOL_WINDOWS])
            d.append(for_group(sums) * inv_count - ext[t])
        z = cg * v
        zext = [sc_ref[t] for t in range(sc_ref.shape[0])] + [z[l * nb:(l + 1) * nb]
                                                              for l in range(dec_seq)]
        c = [cw0 * zext[l] + cw1 * zext[l + 1] + cw2 * zext[l + 2] for l in range(dec_seq)]
        finish(jnp.concatenate(d, axis=0), bg * jnp.concatenate(c, axis=0))
        zs_ref[...] = z[tm - zs_ref.shape[0]:]
        ut_ref[0] = u[tm - U_HALO:]
        zt_ref[0] = jnp.zeros(zt_ref.shape[1:], F32)


def _mixin(h, w_in, layer, state_pool_tm, state_conv_tm, w_pool, pool_scale, conv_w,
           w_up, w_down, *, n_prompt_rows, seq):
    h_parts = _parts(h)
    rows, d = sum(p.shape[0] for p in h_parts), h_parts[0].shape[1]
    assert len(h_parts) == 1 or h_parts[0].shape[0] == n_prompt_rows
    tm = ROW_TILE
    n_groups = len(POOL_WINDOWS)
    pw = state_pool_tm.shape[2]
    cwid = state_conv_tm.shape[2]
    gc = pw // n_groups
    assert w_in.shape[2] == pw + 3 * cwid and cwid == pw
    n_tiles = rows // tm
    n_prompt_tiles = n_prompt_rows // tm
    dec_batch = state_pool_tm.shape[1]
    dec_seq = (rows - n_prompt_rows) // dec_batch
    assert n_tiles == n_prompt_tiles + 1 and dec_seq * dec_batch == tm and seq % tm == 0
    n_conv_prev = state_conv_tm.shape[0]
    f = w_up.shape[2]
    n_chunks = f // CAST_CHUNK
    assert n_chunks <= n_groups * n_tiles and f % CAST_CHUNK == 0

    def chunk(g, i):
        return jnp.minimum(g * n_tiles + i, n_chunks - 1)

    def w_in_spec(k):
        return pl.BlockSpec((1, d, gc), lambda g, i: (layer, 0, k * n_groups + g))

    kernel = functools.partial(_mixin_kernel, n_h_parts=len(h_parts),
                               n_prompt_tiles=n_prompt_tiles, tiles_per_seq=seq // tm,
                               dec_batch=dec_batch, dec_seq=dec_seq)
    return pl.pallas_call(
        kernel,
        grid=(n_groups, n_tiles),
        in_specs=_part_specs(h_parts, d, lambda g, i: 0, 1) + [
            w_in_spec(0), w_in_spec(1), w_in_spec(2), w_in_spec(3),
            pl.BlockSpec(state_pool_tm.shape[:2] + (gc,), lambda g, i: (0, 0, g)),
            pl.BlockSpec(state_conv_tm.shape[:2] + (gc,), lambda g, i: (0, 0, g)),
            pl.BlockSpec((1, 1, gc, gc), lambda g, i: (layer, g, 0, 0)),
            pl.BlockSpec((1, gc), lambda g, i: (0, g)),
            pl.BlockSpec((conv_w.shape[0], gc), lambda g, i: (0, g)),
            pl.BlockSpec((1, d, CAST_CHUNK), lambda g, i: (layer, 0, chunk(g, i))),
            pl.BlockSpec((1, CAST_CHUNK, d), lambda g, i: (layer, chunk(g, i), 0)),
        ],
        out_specs=[
            pl.BlockSpec((tm, gc), lambda g, i: (i, g)),
            pl.BlockSpec((tm, gc), lambda g, i: (i, g)),
            pl.BlockSpec((1, U_HALO, gc), lambda g, i: (i, 0, g)),
            pl.BlockSpec((1, Z_HALO, gc), lambda g, i: (i, 0, g)),
            pl.BlockSpec((tm, gc), lambda g, i: (0, g)),
            pl.BlockSpec((n_conv_prev * dec_batch, gc), lambda g, i: (0, g)),
            pl.BlockSpec((d, CAST_CHUNK), lambda g, i: (0, chunk(g, i))),
            pl.BlockSpec((CAST_CHUNK, d), lambda g, i: (chunk(g, i), 0)),
        ],
        out_shape=[
            jax.ShapeDtypeStruct((rows, pw), BF16),
            jax.ShapeDtypeStruct((rows, cwid), BF16),
            jax.ShapeDtypeStruct((n_tiles, U_HALO, pw), F32),
            jax.ShapeDtypeStruct((n_tiles, Z_HALO, cwid), F32),
            jax.ShapeDtypeStruct((tm, pw), F32),
            jax.ShapeDtypeStruct((n_conv_prev * dec_batch, cwid), F32),
            jax.ShapeDtypeStruct((d, f), BF16),
            jax.ShapeDtypeStruct((f, d), BF16),
        ],
        scratch_shapes=[
            pltpu.VMEM((d, 4 * gc), BF16),
            pltpu.VMEM((U_HALO, gc), F32),
            pltpu.VMEM((Z_HALO, gc), F32),
        ],
        compiler_params=_cparams("arbitrary", "arbitrary"),
        name="mixin",
    )(*h_parts, w_in, w_in, w_in, w_in, state_pool_tm, state_conv_tm, w_pool,
      pool_scale.reshape(1, pw), conv_w, w_up, w_down)


def _masked_softmax_rows(s, valid):
    s = jnp.where(valid, s, -jnp.inf)
    m = jnp.max(s, axis=-1, keepdims=True)
    e = jnp.exp(s - m)
    return e * (1.0 / jnp.sum(e, axis=-1, keepdims=True))


def _softmax_rows(s):
    m = jnp.max(s, axis=-1, keepdims=True)
    e = jnp.exp(s - m)
    return e * (1.0 / jnp.sum(e, axis=-1, keepdims=True))


def _attn_prompt_kernel(q_ref, k_ref, v_ref, o_ref):
    hd = k_ref.shape[-1]
    scale = 1.0 / math.sqrt(hd)
    for h in range(N_XHEADS):
        cols = slice(h * hd, (h + 1) * hd)
        s = lax.dot_general(q_ref[:, cols], k_ref[0, 0, h], (((1,), (1,)), ((), ())),
                            preferred_element_type=F32) * scale
        p = _softmax_rows(s).astype(BF16)
        o_ref[:, cols] = jnp.dot(p, v_ref[0, 0, h], preferred_element_type=F32).astype(o_ref.dtype)


def _attn_prompt(q, mk, mv, layer, *, batch, seq):
    d = q.shape[1]
    kv_block = (1, 1) + mk.shape[2:]
    tq = ROW_TILE
    tps = seq // tq
    return pl.pallas_call(
        _attn_prompt_kernel,
        grid=(batch, tps),
        in_specs=[pl.BlockSpec((tq, d), lambda b, t: (b * tps + t, 0)),
                  pl.BlockSpec(kv_block, lambda b, t: (layer, b, 0, 0, 0)),
                  pl.BlockSpec(kv_block, lambda b, t: (layer, b, 0, 0, 0))],
        out_specs=pl.BlockSpec((tq, d), lambda b, t: (b * tps + t, 0)),
        out_shape=jax.ShapeDtypeStruct((batch * seq, d), BF16),
        compiler_params=_cparams("parallel", "parallel"),
        name="attn_prompt",
    )(q, mk, mv)


def _sample_attn_operands(q, k, v):
    lq, d = q.shape
    n_mem, n_heads, hd = k.shape
    assert lq <= SUBLANES_F32
    q8 = jnp.concatenate([q, jnp.zeros((SUBLANES_F32 - lq, d), F32)], axis=0)
    qh = jnp.concatenate([q8[:, h * hd:(h + 1) * hd] for h in range(n_heads)], axis=0)
    k2 = k.reshape(n_mem * n_heads, hd).astype(BF16)
    v2 = v.reshape(n_mem * n_heads, hd).astype(BF16)
    return qh.astype(BF16), k2, v2


def _sample_attn_probs(qh, k2, n_heads):
    n_rows, hd = qh.shape
    n_keys = k2.shape[0]
    head_of_row = lax.broadcasted_iota(jnp.int32, (n_rows, n_keys), 0) // SUBLANES_F32
    head_of_key = lax.broadcasted_iota(jnp.int32, (n_rows, n_keys), 1) % n_heads
    s = lax.dot_general(qh, k2, (((1,), (1,)), ((), ())),
                        preferred_element_type=F32) * (1.0 / math.sqrt(hd))
    return _masked_softmax_rows(s, head_of_row == head_of_key).astype(BF16)


def _sample_attn_output(p, v2, lq, n_heads):
    o = jnp.dot(p, v2, preferred_element_type=F32)
    return jnp.concatenate(
        [o[h * SUBLANES_F32:h * SUBLANES_F32 + lq] for h in range(n_heads)], axis=1)


def _mlp_kernel(*refs, emit_x, with_attn):
    refs = list(refs)
    h_ref, x_ref, wu_ref, wd_ref, g_ref = (refs.pop(0) for _ in range(5))
    q_ref, k_ref, v_ref = (refs.pop(0) for _ in range(3)) if with_attn else (None,) * 3
    xo_ref = refs.pop(0) if emit_x else None
    ho_ref = refs.pop(0)
    os_ref = refs.pop(0) if with_attn else None
    acc_ref = refs.pop(0)
    j = pl.program_id(1)

    @pl.when(j == 0)
    def _():
        acc_ref[...] = x_ref[...]

    a = jnp.dot(h_ref[...], wu_ref[...], preferred_element_type=F32)
    if with_attn:
        n_heads = k_ref.shape[3]
        qh, k2, v2 = _sample_attn_operands(q_ref[0], k_ref[0, 0], v_ref[0, 0])
        p = _sample_attn_probs(qh, k2, n_heads)
    a = jnp.square(jnp.maximum(a, 0.0)).astype(BF16)
    acc_ref[...] += jnp.dot(a, wd_ref[...], preferred_element_type=F32)
    if with_attn:
        os_ref[0] = _sample_attn_output(p, v2, q_ref.shape[1], n_heads)

    @pl.when(j == pl.num_programs(1) - 1)
    def _():
        x = acc_ref[...]
        if emit_x:
            xo_ref[...] = x
        ho_ref[...] = _rms_rows(x, g_ref[...]).astype(ho_ref.dtype)


MLP_HIDDEN_TILE = 1024


def _mlp(h, x, w_up, w_down, norm_g, *, norm_dtype=BF16, emit_x=True, sample_attn=None):
    m, d = x.shape
    f = w_up.shape[1]
    tf = MLP_HIDDEN_TILE
    n_tiles, n_chunks = m // ROW_TILE, f // tf
    row_spec = pl.BlockSpec((ROW_TILE, d), lambda i, j: (i, 0))
    in_specs = [row_spec, row_spec,
                pl.BlockSpec((d, tf), lambda i, j: (0, j)),
                pl.BlockSpec((tf, d), lambda i, j: (j, 0)),
                pl.BlockSpec((1, d), lambda i, j: (0, 0))]
    args = [h, x, w_up, w_down, norm_g.reshape(1, d)]
    out_specs, out_shape = [], []
    if emit_x:
        out_specs.append(row_spec)
        out_shape.append(jax.ShapeDtypeStruct((m, d), F32))
    out_specs.append(row_spec)
    out_shape.append(jax.ShapeDtypeStruct((m, d), norm_dtype))
    if sample_attn is not None:
        q, cache_k, cache_v, layer = sample_attn
        nb, lq, _ = q.shape
        assert nb == n_tiles * n_chunks
        kv_block = (1, 1) + cache_k.shape[2:]
        in_specs += [pl.BlockSpec((1, lq, d), lambda i, j: (i * n_chunks + j, 0, 0)),
                     pl.BlockSpec(kv_block, lambda i, j: (layer, i * n_chunks + j, 0, 0, 0)),
                     pl.BlockSpec(kv_block, lambda i, j: (layer, i * n_chunks + j, 0, 0, 0))]
        args += [q, cache_k, cache_v]
        out_specs.append(pl.BlockSpec((1, lq, d), lambda i, j: (i * n_chunks + j, 0, 0)))
        out_shape.append(jax.ShapeDtypeStruct((nb, lq, d), F32))
    return pl.pallas_call(
        functools.partial(_mlp_kernel, emit_x=emit_x, with_attn=sample_attn is not None),
        grid=(n_tiles, n_chunks),
        in_specs=in_specs,
        out_specs=out_specs,
        out_shape=out_shape,
        scratch_shapes=[pltpu.VMEM((ROW_TILE, d), F32)],
        compiler_params=_cparams("arbitrary", "arbitrary"),
        name="mlp",
    )(*args)


def kernel(x_prompt, x_sample, mem_prompt, cache_mem_k, cache_mem_v, state_pool, state_conv,
           norm_mix, w_in, w_pool, pool_scale, conv_w, w_out, norm_attn, norm_mem, w_q, w_k, w_v,
           w_o, norm_mlp, w_up, w_down, norm_final):
    batch, seq, d = x_prompt.shape
    nb, lq, _ = x_sample.shape
    depth = w_in.shape[0]
    pool_buf, pw = state_pool.shape[2], state_pool.shape[3]
    conv_buf, cwid = state_conv.shape[2], state_conv.shape[3]
    n_prompt_rows = batch * seq
    tiles_per_seq = seq // ROW_TILE
    assert nb * lq == ROW_TILE and pool_buf < U_HALO and conv_buf == CONV_K - 1 <= Z_HALO
    assert cache_mem_k.shape[3] == N_XHEADS

    mem_k, mem_k_heads = _mem_proj(mem_prompt, norm_mem, w_k)
    mem_v, mem_v_heads = _mem_proj(mem_prompt, norm_mem, w_v)

    x = [x_prompt.reshape(n_prompt_rows, d), x_sample.transpose(1, 0, 2).reshape(lq * nb, d)]
    h = _rms(x, norm_mix[0])
    pool_p, conv_p, pool_s, conv_s = [], [], [], []
    for l in range(depth):
        sp_tm = state_pool[l].transpose(1, 0, 2)
        sc_tm = state_conv[l].transpose(1, 0, 2)
        ya, yb, u_tail, z_tail, u_s, z_s, w_up_b, w_down_b = _mixin(
            h, w_in, l, sp_tm, sc_tm, w_pool, pool_scale[l], conv_w[l], w_up, w_down,
            n_prompt_rows=n_prompt_rows, seq=seq)
        x, h = _mm([ya, yb], w_out, l, res=x, norm_g=norm_attn[l])
        last_tiles = slice(tiles_per_seq - 1, batch * tiles_per_seq, tiles_per_seq)
        pool_p.append(u_tail[last_tiles, U_HALO - pool_buf:])
        conv_p.append(z_tail[last_tiles, Z_HALO - conv_buf:])
        u_s = u_s.reshape(lq, nb, pw).transpose(1, 0, 2)
        pool_s.append(jnp.concatenate([state_pool[l], u_s], axis=1)[:, -pool_buf:])
        conv_s.append(z_s.reshape(conv_buf, nb, cwid).transpose(1, 0, 2))
        q = _mm([h], w_q, l, out_dtype=BF16)
        o_p = _attn_prompt(q, mem_k_heads, mem_v_heads, l, batch=batch, seq=seq)
        q_s = q[n_prompt_rows:].astype(F32).reshape(lq, nb, d).transpose(1, 0, 2)
        last = l == depth - 1
        g_next = norm_final if last else norm_mix[l + 1]
        mlp_opts = dict(norm_dtype=F32 if last else BF16, emit_x=not last)
        x_p, h_p = _mm([o_p], w_o, l, res=x, norm_g=norm_mlp[l])
        *xh_p, o_s = _mlp(h_p, x_p, w_up_b, w_down_b, g_next, **mlp_opts,
                          sample_attn=(q_s, cache_mem_k, cache_mem_v, l))
        o_s = o_s.transpose(1, 0, 2).reshape(lq * nb, d).astype(BF16)
        x_s, h_s = _mm([o_s], w_o, l, res=x[n_prompt_rows:], norm_g=norm_mlp[l])
        xh_s = _mlp(h_s, x_s, w_up_b, w_down_b, g_next, **mlp_opts)
        if last:
            (y_p,), (y_s,) = xh_p, xh_s
        else:
            (x_p, h_p), (x_s, h_s) = xh_p, xh_s
            x, h = [x_p, x_s], [h_p, h_s]

    y_prompt = y_p.reshape(batch, seq, d)
    y_sample = y_s.reshape(lq, nb, d).transpose(1, 0, 2)
    return (y_prompt, y_sample, jnp.stack(pool_p), jnp.stack(conv_p), mem_k, mem_v,
            jnp.stack(pool_s), jnp.stack(conv_s))
```

```python
import functools
import math

import jax
import jax.numpy as jnp
from jax import lax
from jax.experimental import pallas as pl
from jax.experimental.pallas import tpu as pltpu

F32 = jnp.float32
BF16 = jnp.bfloat16

PAST_LEN = 16384
POOL_WINDOWS = (2, 4, 8, 16)
CONV_K = 3
N_XHEADS = 4
EPS = 1e-6

V7X_VMEM_BYTES = 64 * 1024 * 1024
VMEM_LIMIT_BYTES = V7X_VMEM_BYTES - 8 * 1024 * 1024
SUBLANES_F32 = 8
LANES = 128

ROW_TILE = 512


def _cparams(*sem):
    return pltpu.CompilerParams(dimension_semantics=sem, vmem_limit_bytes=VMEM_LIMIT_BYTES)


def _rms_rows(x, g):
    ms = jnp.mean(x * x, axis=-1, keepdims=True)
    return x * lax.rsqrt(ms + EPS) * g


def _parts(x):
    return list(x) if isinstance(x, (list, tuple)) else [x]


def _part_specs(parts, cols, col_index, row_axis):
    specs, start = [], 0
    for p in parts:
        n = p.shape[0] // ROW_TILE
        assert p.shape[0] % ROW_TILE == 0

        def index_map(*ids, start=start, n=n):
            i = ids[row_axis]
            return (jnp.clip(i - start, 0, n - 1), col_index(*ids))

        mode = {"pipeline_mode": pl.Buffered(1)} if n == 1 and len(parts) > 1 else {}
        specs.append(pl.BlockSpec((ROW_TILE, cols), index_map, **mode))
        start += n
    return specs


def _read_parts(refs, i, first_part_tiles):
    if len(refs) == 1:
        return refs[0][...]
    return jnp.where(i < first_part_tiles, refs[0][...], refs[1][...])


def _rms_kernel(*refs, n_parts, first_part_tiles):
    x_refs, g_ref, o_ref = refs[:n_parts], refs[n_parts], refs[n_parts + 1]
    x = _read_parts(x_refs, pl.program_id(0), first_part_tiles)
    o_ref[...] = _rms_rows(x, g_ref[...]).astype(o_ref.dtype)


def _rms(x, g, out_dtype=BF16):
    parts = _parts(x)
    rows = sum(p.shape[0] for p in parts)
    d = parts[0].shape[1]
    return pl.pallas_call(
        functools.partial(_rms_kernel, n_parts=len(parts),
                          first_part_tiles=parts[0].shape[0] // ROW_TILE),
        grid=(rows // ROW_TILE,),
        in_specs=_part_specs(parts, d, lambda i: 0, 0) + [pl.BlockSpec((1, d), lambda i: (0, 0))],
        out_specs=pl.BlockSpec((ROW_TILE, d), lambda i: (i, 0)),
        out_shape=jax.ShapeDtypeStruct((rows, d), out_dtype),
        compiler_params=_cparams("parallel"),
        name="rms",
    )(*parts, g.reshape(1, d))


def _mm_kernel(*refs, kpart_sizes, n_res, chained, first_part_tiles):
    refs = list(refs)
    a_refs = [[refs.pop(0) for _ in range(n)] for n in kpart_sizes]
    w_ref = refs.pop(0)
    res_refs = [refs.pop(0) for _ in range(n_res)]
    g_ref = refs.pop(0)
    w2_ref = refs.pop(0) if chained else None
    o_ref, h_ref = refs
    i = pl.program_id(0)

    acc, k0 = None, 0
    for part_refs in a_refs:
        a = _read_parts(part_refs, i, first_part_tiles)
        t = jnp.dot(a, w_ref[k0:k0 + a.shape[1], :], preferred_element_type=F32)
        acc = t if acc is None else acc + t
        k0 += a.shape[1]
    acc = acc + _read_parts(res_refs, i, first_part_tiles)
    o_ref[...] = acc
    h = _rms_rows(acc, g_ref[...]).astype(BF16)
    if chained:
        h_ref[...] = jnp.dot(h, w2_ref[...], preferred_element_type=F32).astype(h_ref.dtype)
    else:
        h_ref[...] = h


def _mm(a_kparts, w, *, res, norm_g, w2=None):
    a_kparts = [_parts(p) for p in a_kparts]
    res_parts = _parts(res)
    m = sum(p.shape[0] for p in a_kparts[0])
    k, n = w.shape
    assert sum(ps[0].shape[1] for ps in a_kparts) == k
    split = {ps[0].shape[0] // ROW_TILE for ps in a_kparts + [res_parts] if len(ps) == 2}
    assert len(split) <= 1
    first_part_tiles = split.pop() if split else m // ROW_TILE
    resident = dict(pipeline_mode=pl.Buffered(1))
    in_specs, args = [], []
    for ps in a_kparts:
        in_specs += _part_specs(ps, ps[0].shape[1], lambda i: 0, 0)
        args += ps
    in_specs.append(pl.BlockSpec((k, n), lambda i: (0, 0), **resident))
    in_specs += _part_specs(res_parts, n, lambda i: 0, 0)
    in_specs.append(pl.BlockSpec((1, n), lambda i: (0, 0)))
    args += [w] + res_parts + [norm_g.reshape(1, n)]
    n2 = n
    if w2 is not None:
        n2 = w2.shape[1]
        in_specs.append(pl.BlockSpec((n, n2), lambda i: (0, 0), **resident))
        args.append(w2)
    return pl.pallas_call(
        functools.partial(_mm_kernel, kpart_sizes=tuple(len(ps) for ps in a_kparts),
                          n_res=len(res_parts), chained=w2 is not None,
                          first_part_tiles=first_part_tiles),
        grid=(m // ROW_TILE,),
        in_specs=in_specs,
        out_specs=[pl.BlockSpec((ROW_TILE, n), lambda i: (i, 0)),
                   pl.BlockSpec((ROW_TILE, n2), lambda i: (i, 0))],
        out_shape=[jax.ShapeDtypeStruct((m, n), F32), jax.ShapeDtypeStruct((m, n2), BF16)],
        compiler_params=_cparams("parallel"),
        name="mm",
    )(*args)


MEM_ROW_TILE = 256


def _mem_proj_kernel(m_ref, g_ref, w_ref, o_ref, ob_ref, wb_ref):
    @pl.when(pl.program_id(1) == 0)
    def _():
        wb_ref[...] = w_ref[0].astype(BF16)

    h = _rms_rows(m_ref[...], g_ref[0]).astype(BF16)
    kv = jnp.dot(h, wb_ref[...], preferred_element_type=F32)
    hd = o_ref.shape[-1]
    for head in range(N_XHEADS):
        o_ref[0, 0, :, head, :] = kv[:, head * hd:(head + 1) * hd]
        ob_ref[0, 0, head] = kv[:, head * hd:(head + 1) * hd].astype(BF16)


def _mem_proj(mem, norm_mem, w):
    batch, n_mem, d = mem.shape
    depth = w.shape[0]
    tm = MEM_ROW_TILE
    assert n_mem % tm == 0
    tiles_per_seq = n_mem // tm
    hd = d // N_XHEADS
    return pl.pallas_call(
        _mem_proj_kernel,
        grid=(depth, batch * tiles_per_seq),
        in_specs=[pl.BlockSpec((tm, d), lambda l, i: (i, 0)),
                  pl.BlockSpec((1, 1, d), lambda l, i: (l, 0, 0)),
                  pl.BlockSpec((1, d, d), lambda l, i: (l, 0, 0))],
        out_specs=[pl.BlockSpec((1, 1, tm, N_XHEADS, hd),
                                lambda l, i: (l, i // tiles_per_seq, i % tiles_per_seq, 0, 0)),
                   pl.BlockSpec((1, 1, N_XHEADS, tm, hd),
                                lambda l, i: (l, i // tiles_per_seq, 0, i % tiles_per_seq, 0))],
        out_shape=[jax.ShapeDtypeStruct((depth, batch, n_mem, N_XHEADS, hd), F32),
                   jax.ShapeDtypeStruct((depth, batch, N_XHEADS, n_mem, hd), BF16)],
        scratch_shapes=[pltpu.VMEM((d, d), BF16)],
        compiler_params=_cparams("arbitrary", "arbitrary"),
        name="mem_proj",
    )(mem.reshape(batch * n_mem, d), norm_mem.reshape(depth, 1, d), w)


U_HALO = 16
Z_HALO = 8
CAST_STEPS = 64


def _mixin_kernel(*refs, n_h_parts, n_cast, n_prompt_tiles, tiles_per_seq, dec_batch, dec_seq):
    refs = list(refs)
    h_refs = [refs.pop(0) for _ in range(n_h_parts)]
    wu_ref, wb_ref, wc_ref, wv_ref, sp_ref, sc_ref, wp_ref, ps_ref, cw_ref = (
        refs.pop(0) for _ in range(9))
    cast_src = [refs.pop(0) for _ in range(n_cast)]
    ya_ref, yb_ref, ut_ref, zt_ref, us_ref, zs_ref = (refs.pop(0) for _ in range(6))
    cast_dst = [refs.pop(0) for _ in range(n_cast)]
    wcat_ref, hu_ref, hz_ref = refs
    tm = ROW_TILE
    g, i = pl.program_id(0), pl.program_id(1)
    gc = hu_ref.shape[1]
    cw0, cw1, cw2 = cw_ref[0:1, :], cw_ref[1:2, :], cw_ref[2:3, :]

    def for_group(values):
        out = values[-1]
        for gg in range(len(values) - 2, -1, -1):
            out = jnp.where(g == gg, values[gg], out)
        return out

    def project(h_ref):
        for src_ref, dst_ref in zip(cast_src, cast_dst):
            dst_ref[...] = src_ref[0].astype(BF16)
        pj = jnp.dot(h_ref[...], wcat_ref[...], preferred_element_type=F32)
        return pj[:, 0:gc], pj[:, gc:2 * gc], pj[:, 2 * gc:3 * gc], pj[:, 3 * gc:4 * gc]

    def finish(d, yb):
        ya = jnp.dot(d.astype(BF16), wp_ref[0, 0].astype(BF16), preferred_element_type=F32)
        ya_ref[...] = (ya * ps_ref[...]).astype(BF16)
        yb_ref[...] = yb.astype(BF16)

    @pl.when(i == 0)
    def _():
        for k, w_ref in enumerate((wu_ref, wb_ref, wc_ref, wv_ref)):
            wcat_ref[:, k * gc:(k + 1) * gc] = w_ref[0].astype(BF16)

    @pl.when(i < n_prompt_tiles)
    def _prompt():
        start = (i % tiles_per_seq) * tm

        @pl.when(start == 0)
        def _():
            hu_ref[...] = jnp.zeros(hu_ref.shape, F32)
            hz_ref[...] = jnp.zeros(hz_ref.shape, F32)

        u, bg, cg, v = project(h_refs[0])
        ext = jnp.concatenate([hu_ref[...], u], axis=0)
        sums, width = [], 1
        for w in POOL_WINDOWS:
            while width < w:
                ext = ext + pltpu.roll(ext, width, 0)
                width *= 2
            sums.append(ext)
        s = for_group(sums)[U_HALO:]
        w_here = for_group([jnp.int32(w) for w in POOL_WINDOWS])
        inv_w = for_group([jnp.float32(1.0 / w) for w in POOL_WINDOWS])
        pos = start + lax.broadcasted_iota(jnp.int32, (U_HALO, 1), 0)
        count = jnp.minimum(w_here, pos + 1).astype(F32)
        mean = jnp.concatenate([s[:U_HALO] / count, s[U_HALO:] * inv_w], axis=0)
        z = cg * v
        zext = jnp.concatenate([hz_ref[...], z], axis=0)
        c = (cw0 * pltpu.roll(zext, 2, 0)[Z_HALO:] + cw1 * pltpu.roll(zext, 1, 0)[Z_HALO:]
             + cw2 * z)
        finish(mean - u, bg * c)
        ut_ref[0] = u[tm - U_HALO:]
        zt_ref[0] = z[tm - Z_HALO:]
        hu_ref[...] = u[tm - U_HALO:]
        hz_ref[...] = z[tm - Z_HALO:]

    @pl.when(i >= n_prompt_tiles)
    def _sample():
        nb = dec_batch
        n_prev = sp_ref.shape[0]
        u, bg, cg, v = project(h_refs[-1])
        us_ref[...] = u
        ext = [sp_ref[t] for t in range(n_prev)] + [u[l * nb:(l + 1) * nb] for l in range(dec_seq)]
        d = []
        for l in range(dec_seq):
            t = n_prev + l
            run, sums = ext[t], []
            for j in range(1, max(POOL_WINDOWS)):
                if j in POOL_WINDOWS:
                    sums.append(run)
                run = run + ext[t - j]
            sums.append(run)
            inv_count = for_group([jnp.float32(1.0 / min(w, PAST_LEN + l + 1))
                                   for w in POOL_WINDOWS])
            d.append(for_group(sums) * inv_count - ext[t])
        z = cg * v
        zext = [sc_ref[t] for t in range(sc_ref.shape[0])] + [z[l * nb:(l + 1) * nb]
                                                              for l in range(dec_seq)]
        c = [cw0 * zext[l] + cw1 * zext[l + 1] + cw2 * zext[l + 2] for l in range(dec_seq)]
        finish(jnp.concatenate(d, axis=0), bg * jnp.concatenate(c, axis=0))
        zs_ref[...] = z[tm - zs_ref.shape[0]:]
        ut_ref[0] = u[tm - U_HALO:]
        zt_ref[0] = jnp.zeros(zt_ref.shape[1:], F32)


def _mixin(h, w_in, layer, state_pool_tm, state_conv_tm, w_pool, pool_scale, conv_w,
           to_bf16, *, n_prompt_rows, seq):
    h_parts = _parts(h)
    rows, d = sum(p.shape[0] for p in h_parts), h_parts[0].shape[1]
    assert len(h_parts) == 1 or h_parts[0].shape[0] == n_prompt_rows
    tm = ROW_TILE
    n_groups = len(POOL_WINDOWS)
    pw = state_pool_tm.shape[2]
    cwid = state_conv_tm.shape[2]
    gc = pw // n_groups
    assert w_in.shape[2] == pw + 3 * cwid and cwid == pw
    n_tiles = rows // tm
    n_prompt_tiles = n_prompt_rows // tm
    dec_batch = state_pool_tm.shape[1]
    dec_seq = (rows - n_prompt_rows) // dec_batch
    assert n_tiles == n_prompt_tiles + 1 and dec_seq * dec_batch == tm and seq % tm == 0
    n_conv_prev = state_conv_tm.shape[0]
    assert CAST_STEPS <= n_groups * n_tiles
    cast_rows = [w.shape[1] // CAST_STEPS for w in to_bf16]
    assert all(w.shape[1] % CAST_STEPS == 0 and r % (2 * SUBLANES_F32) == 0
               for w, r in zip(to_bf16, cast_rows))

    def chunk(g, i):
        return jnp.minimum(g * n_tiles + i, CAST_STEPS - 1)

    def w_in_spec(k):
        return pl.BlockSpec((1, d, gc), lambda g, i: (layer, 0, k * n_groups + g))

    kernel = functools.partial(_mixin_kernel, n_h_parts=len(h_parts), n_cast=len(to_bf16),
                               n_prompt_tiles=n_prompt_tiles, tiles_per_seq=seq // tm,
                               dec_batch=dec_batch, dec_seq=dec_seq)
    cast_in_specs = [pl.BlockSpec((1, r, w.shape[2]), lambda g, i: (layer, chunk(g, i), 0))
                     for w, r in zip(to_bf16, cast_rows)]
    cast_out_specs = [pl.BlockSpec((r, w.shape[2]), lambda g, i: (chunk(g, i), 0))
                      for w, r in zip(to_bf16, cast_rows)]
    cast_out_shape = [jax.ShapeDtypeStruct(w.shape[1:], BF16) for w in to_bf16]
    return pl.pallas_call(
        kernel,
        grid=(n_groups, n_tiles),
        in_specs=_part_specs(h_parts, d, lambda g, i: 0, 1) + [
            w_in_spec(0), w_in_spec(1), w_in_spec(2), w_in_spec(3),
            pl.BlockSpec(state_pool_tm.shape[:2] + (gc,), lambda g, i: (0, 0, g)),
            pl.BlockSpec(state_conv_tm.shape[:2] + (gc,), lambda g, i: (0, 0, g)),
            pl.BlockSpec((1, 1, gc, gc), lambda g, i: (layer, g, 0, 0)),
            pl.BlockSpec((1, gc), lambda g, i: (0, g)),
            pl.BlockSpec((conv_w.shape[0], gc), lambda g, i: (0, g)),
        ] + cast_in_specs,
        out_specs=[
            pl.BlockSpec((tm, gc), lambda g, i: (i, g)),
            pl.BlockSpec((tm, gc), lambda g, i: (i, g)),
            pl.BlockSpec((1, U_HALO, gc), lambda g, i: (i, 0, g)),
            pl.BlockSpec((1, Z_HALO, gc), lambda g, i: (i, 0, g)),
            pl.BlockSpec((tm, gc), lambda g, i: (0, g)),
            pl.BlockSpec((n_conv_prev * dec_batch, gc), lambda g, i: (0, g)),
        ] + cast_out_specs,
        out_shape=[
            jax.ShapeDtypeStruct((rows, pw), BF16),
            jax.ShapeDtypeStruct((rows, cwid), BF16),
            jax.ShapeDtypeStruct((n_tiles, U_HALO, pw), F32),
            jax.ShapeDtypeStruct((n_tiles, Z_HALO, cwid), F32),
            jax.ShapeDtypeStruct((tm, pw), F32),
            jax.ShapeDtypeStruct((n_conv_prev * dec_batch, cwid), F32),
        ] + cast_out_shape,
        scratch_shapes=[
            pltpu.VMEM((d, 4 * gc), BF16),
            pltpu.VMEM((U_HALO, gc), F32),
            pltpu.VMEM((Z_HALO, gc), F32),
        ],
        compiler_params=_cparams("arbitrary", "arbitrary"),
        name="mixin",
    )(*h_parts, w_in, w_in, w_in, w_in, state_pool_tm, state_conv_tm, w_pool,
      pool_scale.reshape(1, pw), conv_w, *to_bf16)


def _masked_softmax_rows(s, valid):
    s = jnp.where(valid, s, -jnp.inf)
    m = jnp.max(s, axis=-1, keepdims=True)
    e = jnp.exp(s - m)
    return e * (1.0 / jnp.sum(e, axis=-1, keepdims=True))


def _softmax_rows(s):
    m = jnp.max(s, axis=-1, keepdims=True)
    e = jnp.exp(s - m)
    return e * (1.0 / jnp.sum(e, axis=-1, keepdims=True))


def _attn_prompt_kernel(q_ref, k_ref, v_ref, wo_ref, x_ref, g_ref, xo_ref, ho_ref,
                        o_even_ref, o_odd_ref, *, n_tiles):
    t = pl.program_id(0)
    hd = k_ref.shape[-1]
    scale = 1.0 / math.sqrt(hd)

    def step(o_new_ref, o_old_ref):
        out = []
        for c in range(N_XHEADS):
            cols = slice(c * hd, (c + 1) * hd)
            if o_new_ref is not None:
                s = lax.dot_general(q_ref[:, cols], k_ref[0, 0, c], (((1,), (1,)), ((), ())),
                                    preferred_element_type=F32) * scale
            if o_old_ref is not None:
                out.append(jnp.dot(o_old_ref[...], wo_ref[:, cols], preferred_element_type=F32))
            if o_new_ref is not None:
                p = _softmax_rows(s).astype(BF16)
                o_new_ref[:, cols] = jnp.dot(p, v_ref[0, 0, c],
                                             preferred_element_type=F32).astype(BF16)
        if o_old_ref is not None:
            acc = x_ref[...] + jnp.concatenate(out, axis=1)
            xo_ref[...] = acc
            ho_ref[...] = _rms_rows(acc, g_ref[...]).astype(ho_ref.dtype)

    last_ref = o_odd_ref if (n_tiles - 1) % 2 else o_even_ref
    pl.when(t == 0)(lambda: step(o_even_ref, None))
    pl.when((t > 0) & (t < n_tiles) & (t % 2 == 1))(lambda: step(o_odd_ref, o_even_ref))
    pl.when((t > 0) & (t < n_tiles) & (t % 2 == 0))(lambda: step(o_even_ref, o_odd_ref))
    pl.when(t == n_tiles)(lambda: step(None, last_ref))


def _attn_prompt(q, mk, mv, layer, w_o, x, norm_g, *, batch, seq):
    d = q.shape[1]
    kv_block = (1, 1) + mk.shape[2:]
    tq = ROW_TILE
    tps = seq // tq
    n_tiles = batch * tps

    def attended(t):
        return jnp.minimum(t, n_tiles - 1)

    def projected(t):
        return jnp.maximum(t - 1, 0)

    out_spec = pl.BlockSpec((tq, d), lambda t: (projected(t), 0))
    return pl.pallas_call(
        functools.partial(_attn_prompt_kernel, n_tiles=n_tiles),
        grid=(n_tiles + 1,),
        in_specs=[pl.BlockSpec((tq, d), lambda t: (attended(t), 0)),
                  pl.BlockSpec(kv_block, lambda t: (layer, attended(t) // tps, 0, 0, 0)),
                  pl.BlockSpec(kv_block, lambda t: (layer, attended(t) // tps, 0, 0, 0)),
                  pl.BlockSpec((d, d), lambda t: (0, 0), pipeline_mode=pl.Buffered(1)),
                  out_spec,
                  pl.BlockSpec((1, d), lambda t: (0, 0))],
        out_specs=[out_spec, out_spec],
        out_shape=[jax.ShapeDtypeStruct((n_tiles * tq, d), F32),
                   jax.ShapeDtypeStruct((n_tiles * tq, d), BF16)],
        scratch_shapes=[pltpu.VMEM((tq, d), BF16), pltpu.VMEM((tq, d), BF16)],
        compiler_params=_cparams("arbitrary"),
        name="attn_prompt",
    )(q, mk, mv, w_o, x, norm_g.reshape(1, d))


def _sample_attn_operands(q, k, v):
    lq, d = q.shape
    n_mem, n_heads, hd = k.shape
    assert lq <= SUBLANES_F32
    q8 = jnp.concatenate([q, jnp.zeros((SUBLANES_F32 - lq, d), F32)], axis=0)
    qh = jnp.concatenate([q8[:, h * hd:(h + 1) * hd] for h in range(n_heads)], axis=0)
    k2 = k.reshape(n_mem * n_heads, hd).astype(BF16)
    v2 = v.reshape(n_mem * n_heads, hd).astype(BF16)
    return qh.astype(BF16), k2, v2


def _sample_attn_probs(qh, k2, n_heads):
    n_rows, hd = qh.shape
    n_keys = k2.shape[0]
    head_of_row = lax.broadcasted_iota(jnp.int32, (n_rows, n_keys), 0) // SUBLANES_F32
    head_of_key = lax.broadcasted_iota(jnp.int32, (n_rows, n_keys), 1) % n_heads
    s = lax.dot_general(qh, k2, (((1,), (1,)), ((), ())),
                        preferred_element_type=F32) * (1.0 / math.sqrt(hd))
    return _masked_softmax_rows(s, head_of_row == head_of_key).astype(BF16)


def _sample_attn_output(p, v2, lq, n_heads):
    o = jnp.dot(p, v2, preferred_element_type=F32)
    return jnp.concatenate(
        [o[h * SUBLANES_F32:h * SUBLANES_F32 + lq] for h in range(n_heads)], axis=1)


def _mlp_kernel(*refs, emit_x, with_attn):
    refs = list(refs)
    h_ref, x_ref, wu_ref, wd_ref, g_ref = (refs.pop(0) for _ in range(5))
    q_ref, k_ref, v_ref = (refs.pop(0) for _ in range(3)) if with_attn else (None,) * 3
    xo_ref = refs.pop(0) if emit_x else None
    ho_ref = refs.pop(0)
    os_ref = refs.pop(0) if with_attn else None
    acc_ref = refs.pop(0)
    j = pl.program_id(1)

    @pl.when(j == 0)
    def _():
        acc_ref[...] = x_ref[...]

    a = jnp.dot(h_ref[...], wu_ref[...], preferred_element_type=F32)
    if with_attn:
        n_heads = k_ref.shape[3]
        qh, k2, v2 = _sample_attn_operands(q_ref[0], k_ref[0, 0], v_ref[0, 0])
        p = _sample_attn_probs(qh, k2, n_heads)
    a = jnp.square(jnp.maximum(a, 0.0)).astype(BF16)
    acc_ref[...] += jnp.dot(a, wd_ref[...], preferred_element_type=F32)
    if with_attn:
        os_ref[0] = _sample_attn_output(p, v2, q_ref.shape[1], n_heads)

    @pl.when(j == pl.num_programs(1) - 1)
    def _():
        x = acc_ref[...]
        if emit_x:
            xo_ref[...] = x
        ho_ref[...] = _rms_rows(x, g_ref[...]).astype(ho_ref.dtype)


MLP_HIDDEN_TILE = 1024


def _mlp(h, x, w_up, w_down, norm_g, *, norm_dtype=BF16, emit_x=True, sample_attn=None):
    m, d = x.shape
    f = w_up.shape[1]
    tf = MLP_HIDDEN_TILE
    n_tiles, n_chunks = m // ROW_TILE, f // tf
    row_spec = pl.BlockSpec((ROW_TILE, d), lambda i, j: (i, 0))
    in_specs = [row_spec, row_spec,
                pl.BlockSpec((d, tf), lambda i, j: (0, j)),
                pl.BlockSpec((tf, d), lambda i, j: (j, 0)),
                pl.BlockSpec((1, d), lambda i, j: (0, 0))]
    args = [h, x, w_up, w_down, norm_g.reshape(1, d)]
    out_specs, out_shape = [], []
    if emit_x:
        out_specs.append(row_spec)
        out_shape.append(jax.ShapeDtypeStruct((m, d), F32))
    out_specs.append(row_spec)
    out_shape.append(jax.ShapeDtypeStruct((m, d), norm_dtype))
    if sample_attn is not None:
        q, cache_k, cache_v, layer = sample_attn
        nb, lq, _ = q.shape
        assert nb == n_tiles * n_chunks
        kv_block = (1, 1) + cache_k.shape[2:]
        in_specs += [pl.BlockSpec((1, lq, d), lambda i, j: (i * n_chunks + j, 0, 0)),
                     pl.BlockSpec(kv_block, lambda i, j: (layer, i * n_chunks + j, 0, 0, 0)),
                     pl.BlockSpec(kv_block, lambda i, j: (layer, i * n_chunks + j, 0, 0, 0))]
        args += [q, cache_k, cache_v]
        out_specs.append(pl.BlockSpec((1, lq, d), lambda i, j: (i * n_chunks + j, 0, 0)))
        out_shape.append(jax.ShapeDtypeStruct((nb, lq, d), F32))
    return pl.pallas_call(
        functools.partial(_mlp_kernel, emit_x=emit_x, with_attn=sample_attn is not None),
        grid=(n_tiles, n_chunks),
        in_specs=in_specs,
        out_specs=out_specs,
        out_shape=out_shape,
        scratch_shapes=[pltpu.VMEM((ROW_TILE, d), F32)],
        compiler_params=_cparams("arbitrary", "arbitrary"),
        name="mlp",
    )(*args)


def kernel(x_prompt, x_sample, mem_prompt, cache_mem_k, cache_mem_v, state_pool, state_conv,
           norm_mix, w_in, w_pool, pool_scale, conv_w, w_out, norm_attn, norm_mem, w_q, w_k, w_v,
           w_o, norm_mlp, w_up, w_down, norm_final):
    batch, seq, d = x_prompt.shape
    nb, lq, _ = x_sample.shape
    depth = w_in.shape[0]
    pool_buf, pw = state_pool.shape[2], state_pool.shape[3]
    conv_buf, cwid = state_conv.shape[2], state_conv.shape[3]
    n_prompt_rows = batch * seq
    tiles_per_seq = seq // ROW_TILE
    assert nb * lq == ROW_TILE and pool_buf < U_HALO and conv_buf == CONV_K - 1 <= Z_HALO
    assert cache_mem_k.shape[3] == N_XHEADS

    mem_k, mem_k_heads = _mem_proj(mem_prompt, norm_mem, w_k)
    mem_v, mem_v_heads = _mem_proj(mem_prompt, norm_mem, w_v)

    x = [x_prompt.reshape(n_prompt_rows, d), x_sample.transpose(1, 0, 2).reshape(lq * nb, d)]
    h = _rms(x, norm_mix[0])
    pool_p, conv_p, pool_s, conv_s = [], [], [], []
    for l in range(depth):
        sp_tm = state_pool[l].transpose(1, 0, 2)
        sc_tm = state_conv[l].transpose(1, 0, 2)
        ya, yb, u_tail, z_tail, u_s, z_s, w_out_b, w_q_b, w_o_b, w_up_b, w_down_b = _mixin(
            h, w_in, l, sp_tm, sc_tm, w_pool, pool_scale[l], conv_w[l],
            [w_out, w_q, w_o, w_up, w_down], n_prompt_rows=n_prompt_rows, seq=seq)
        x, q = _mm([ya, yb], w_out_b, res=x, norm_g=norm_attn[l], w2=w_q_b)
        last_tiles = slice(tiles_per_seq - 1, batch * tiles_per_seq, tiles_per_seq)
        pool_p.append(u_tail[last_tiles, U_HALO - pool_buf:])
        conv_p.append(z_tail[last_tiles, Z_HALO - conv_buf:])
        pool_s.append(jnp.concatenate([sp_tm, u_s.reshape(lq, nb, pw)], axis=0)[-pool_buf:])
        conv_s.append(z_s.reshape(conv_buf, nb, cwid).transpose(1, 0, 2))
        x_p, h_p = _attn_prompt(q, mem_k_heads, mem_v_heads, l, w_o_b, x, norm_mlp[l],
                                batch=batch, seq=seq)
        q_s = q[n_prompt_rows:].astype(F32).reshape(lq, nb, d).transpose(1, 0, 2)
        last = l == depth - 1
        g_next = norm_final if last else norm_mix[l + 1]
        mlp_opts = dict(norm_dtype=F32 if last else BF16, emit_x=not last)
        *xh_p, o_s = _mlp(h_p, x_p, w_up_b, w_down_b, g_next, **mlp_opts,
                          sample_attn=(q_s, cache_mem_k, cache_mem_v, l))
        o_s = o_s.transpose(1, 0, 2).reshape(lq * nb, d).astype(BF16)
        x_s, h_s = _mm([o_s], w_o_b, res=x[n_prompt_rows:], norm_g=norm_mlp[l])
        xh_s = _mlp(h_s, x_s, w_up_b, w_down_b, g_next, **mlp_opts)
        if last:
            (y_p,), (y_s,) = xh_p, xh_s
        else:
            (x_p, h_p), (x_s, h_s) = xh_p, xh_s
            x, h = [x_p, x_s], [h_p, h_s]

    y_prompt = y_p.reshape(batch, seq, d)
    y_sample = y_s.reshape(lq, nb, d).transpose(1, 0, 2)
    return (y_prompt, y_sample, jnp.stack(pool_p), jnp.stack(conv_p), mem_k, mem_v,
            jnp.stack(pool_s).transpose(0, 2, 1, 3), jnp.stack(conv_s))
```

```python
import functools
import math

import jax
import jax.numpy as jnp
from jax import lax
from jax.experimental import pallas as pl
from jax.experimental.pallas import tpu as pltpu

F32 = jnp.float32
BF16 = jnp.bfloat16

PAST_LEN = 16384
POOL_WINDOWS = (2, 4, 8, 16)
CONV_K = 3
N_XHEADS = 4
EPS = 1e-6

V7X_VMEM_BYTES = 64 * 1024 * 1024
VMEM_LIMIT_BYTES = V7X_VMEM_BYTES - 8 * 1024 * 1024
SUBLANES_F32 = 8
LANES = 128

ROW_TILE = 512


def _cparams(*sem):
    return pltpu.CompilerParams(dimension_semantics=sem, vmem_limit_bytes=VMEM_LIMIT_BYTES)


def _rms_rows(x, g):
    ms = jnp.mean(x * x, axis=-1, keepdims=True)
    return x * lax.rsqrt(ms + EPS) * g


def _parts(x):
    return list(x) if isinstance(x, (list, tuple)) else [x]


def _part_specs(parts, cols, col_index, row_axis):
    specs, start = [], 0
    for p in parts:
        n = p.shape[0] // ROW_TILE
        assert p.shape[0] % ROW_TILE == 0

        def index_map(*ids, start=start, n=n):
            i = ids[row_axis]
            return (jnp.clip(i - start, 0, n - 1), col_index(*ids))

        mode = {"pipeline_mode": pl.Buffered(1)} if n == 1 and len(parts) > 1 else {}
        specs.append(pl.BlockSpec((ROW_TILE, cols), index_map, **mode))
        start += n
    return specs


def _read_parts(refs, i, first_part_tiles):
    if len(refs) == 1:
        return refs[0][...]
    return jnp.where(i < first_part_tiles, refs[0][...], refs[1][...])


def _rms_kernel(*refs, n_parts, first_part_tiles):
    x_refs, g_ref, o_ref = refs[:n_parts], refs[n_parts], refs[n_parts + 1]
    x = _read_parts(x_refs, pl.program_id(0), first_part_tiles)
    o_ref[...] = _rms_rows(x, g_ref[...]).astype(o_ref.dtype)


def _rms(x, g, out_dtype=BF16):
    parts = _parts(x)
    rows = sum(p.shape[0] for p in parts)
    d = parts[0].shape[1]
    return pl.pallas_call(
        functools.partial(_rms_kernel, n_parts=len(parts),
                          first_part_tiles=parts[0].shape[0] // ROW_TILE),
        grid=(rows // ROW_TILE,),
        in_specs=_part_specs(parts, d, lambda i: 0, 0) + [pl.BlockSpec((1, d), lambda i: (0, 0))],
        out_specs=pl.BlockSpec((ROW_TILE, d), lambda i: (i, 0)),
        out_shape=jax.ShapeDtypeStruct((rows, d), out_dtype),
        compiler_params=_cparams("parallel"),
        name="rms",
    )(*parts, g.reshape(1, d))


def _mm_kernel(*refs, kpart_sizes, n_res, chained, first_part_tiles):
    refs = list(refs)
    a_refs = [[refs.pop(0) for _ in range(n)] for n in kpart_sizes]
    w_ref = refs.pop(0)
    res_refs = [refs.pop(0) for _ in range(n_res)]
    g_ref = refs.pop(0)
    w2_ref = refs.pop(0) if chained else None
    o_ref, h_ref = refs
    i = pl.program_id(0)

    acc, k0 = None, 0
    for part_refs in a_refs:
        a = _read_parts(part_refs, i, first_part_tiles)
        t = jnp.dot(a, w_ref[k0:k0 + a.shape[1], :], preferred_element_type=F32)
        acc = t if acc is None else acc + t
        k0 += a.shape[1]
    acc = acc + _read_parts(res_refs, i, first_part_tiles)
    o_ref[...] = acc
    h = _rms_rows(acc, g_ref[...]).astype(BF16)
    if chained:
        h_ref[...] = jnp.dot(h, w2_ref[...], preferred_element_type=F32).astype(h_ref.dtype)
    else:
        h_ref[...] = h


def _mm(a_kparts, w, *, res, norm_g, w2=None):
    a_kparts = [_parts(p) for p in a_kparts]
    res_parts = _parts(res)
    m = sum(p.shape[0] for p in a_kparts[0])
    k, n = w.shape
    assert sum(ps[0].shape[1] for ps in a_kparts) == k
    split = {ps[0].shape[0] // ROW_TILE for ps in a_kparts + [res_parts] if len(ps) == 2}
    assert len(split) <= 1
    first_part_tiles = split.pop() if split else m // ROW_TILE
    resident = dict(pipeline_mode=pl.Buffered(1))
    in_specs, args = [], []
    for ps in a_kparts:
        in_specs += _part_specs(ps, ps[0].shape[1], lambda i: 0, 0)
        args += ps
    in_specs.append(pl.BlockSpec((k, n), lambda i: (0, 0), **resident))
    in_specs += _part_specs(res_parts, n, lambda i: 0, 0)
    in_specs.append(pl.BlockSpec((1, n), lambda i: (0, 0)))
    args += [w] + res_parts + [norm_g.reshape(1, n)]
    n2 = n
    if w2 is not None:
        n2 = w2.shape[1]
        in_specs.append(pl.BlockSpec((n, n2), lambda i: (0, 0), **resident))
        args.append(w2)
    return pl.pallas_call(
        functools.partial(_mm_kernel, kpart_sizes=tuple(len(ps) for ps in a_kparts),
                          n_res=len(res_parts), chained=w2 is not None,
                          first_part_tiles=first_part_tiles),
        grid=(m // ROW_TILE,),
        in_specs=in_specs,
        out_specs=[pl.BlockSpec((ROW_TILE, n), lambda i: (i, 0)),
                   pl.BlockSpec((ROW_TILE, n2), lambda i: (i, 0))],
        out_shape=[jax.ShapeDtypeStruct((m, n), F32), jax.ShapeDtypeStruct((m, n2), BF16)],
        compiler_params=_cparams("parallel"),
        name="mm",
    )(*args)


MEM_ROW_TILE = 256


def _mem_proj_kernel(m_ref, g_ref, w_ref, o_ref, ob_ref, wb_ref):
    @pl.when(pl.program_id(1) == 0)
    def _():
        wb_ref[...] = w_ref[0].astype(BF16)

    h = _rms_rows(m_ref[...], g_ref[0]).astype(BF16)
    kv = jnp.dot(h, wb_ref[...], preferred_element_type=F32)
    hd = o_ref.shape[-1]
    for head in range(N_XHEADS):
        o_ref[0, 0, :, head, :] = kv[:, head * hd:(head + 1) * hd]
        ob_ref[0, 0, head] = kv[:, head * hd:(head + 1) * hd].astype(BF16)


def _mem_proj(mem, norm_mem, w):
    batch, n_mem, d = mem.shape
    depth = w.shape[0]
    tm = MEM_ROW_TILE
    assert n_mem % tm == 0
    tiles_per_seq = n_mem // tm
    hd = d // N_XHEADS
    return pl.pallas_call(
        _mem_proj_kernel,
        grid=(depth, batch * tiles_per_seq),
        in_specs=[pl.BlockSpec((tm, d), lambda l, i: (i, 0)),
                  pl.BlockSpec((1, 1, d), lambda l, i: (l, 0, 0)),
                  pl.BlockSpec((1, d, d), lambda l, i: (l, 0, 0))],
        out_specs=[pl.BlockSpec((1, 1, tm, N_XHEADS, hd),
                                lambda l, i: (l, i // tiles_per_seq, i % tiles_per_seq, 0, 0)),
                   pl.BlockSpec((1, 1, N_XHEADS, tm, hd),
                                lambda l, i: (l, i // tiles_per_seq, 0, i % tiles_per_seq, 0))],
        out_shape=[jax.ShapeDtypeStruct((depth, batch, n_mem, N_XHEADS, hd), F32),
                   jax.ShapeDtypeStruct((depth, batch, N_XHEADS, n_mem, hd), BF16)],
        scratch_shapes=[pltpu.VMEM((d, d), BF16)],
        compiler_params=_cparams("arbitrary", "arbitrary"),
        name="mem_proj",
    )(mem.reshape(batch * n_mem, d), norm_mem.reshape(depth, 1, d), w)


U_HALO = 16
Z_HALO = 8
CAST_STEPS = 64


def _mixin_kernel(*refs, n_h_parts, n_cast, n_prompt_tiles, tiles_per_seq, dec_batch, dec_seq):
    refs = list(refs)
    h_refs = [refs.pop(0) for _ in range(n_h_parts)]
    wu_ref, wb_ref, wc_ref, wv_ref, sp_ref, sc_ref, wp_ref, ps_ref, cw_ref = (
        refs.pop(0) for _ in range(9))
    cast_src = [refs.pop(0) for _ in range(n_cast)]
    ya_ref, yb_ref, ut_ref, zt_ref, us_ref, zs_ref = (refs.pop(0) for _ in range(6))
    cast_dst = [refs.pop(0) for _ in range(n_cast)]
    wcat_ref, hu_ref, hz_ref = refs
    tm = ROW_TILE
    g, i = pl.program_id(0), pl.program_id(1)
    gc = hu_ref.shape[1]
    cw0, cw1, cw2 = cw_ref[0:1, :], cw_ref[1:2, :], cw_ref[2:3, :]

    def for_group(values):
        out = values[-1]
        for gg in range(len(values) - 2, -1, -1):
            out = jnp.where(g == gg, values[gg], out)
        return out

    def project(h_ref):
        for src_ref, dst_ref in zip(cast_src, cast_dst):
            dst_ref[...] = src_ref[0].astype(BF16)
        pj = jnp.dot(h_ref[...], wcat_ref[...], preferred_element_type=F32)
        return pj[:, 0:gc], pj[:, gc:2 * gc], pj[:, 2 * gc:3 * gc], pj[:, 3 * gc:4 * gc]

    def finish(d, yb):
        ya = jnp.dot(d.astype(BF16), wp_ref[0, 0].astype(BF16), preferred_element_type=F32)
        ya_ref[...] = (ya * ps_ref[...]).astype(BF16)
        yb_ref[...] = yb.astype(BF16)

    @pl.when(i == 0)
    def _():
        for k, w_ref in enumerate((wu_ref, wb_ref, wc_ref, wv_ref)):
            wcat_ref[:, k * gc:(k + 1) * gc] = w_ref[0].astype(BF16)

    @pl.when(i < n_prompt_tiles)
    def _prompt():
        start = (i % tiles_per_seq) * tm

        @pl.when(start == 0)
        def _():
            hu_ref[...] = jnp.zeros(hu_ref.shape, F32)
            hz_ref[...] = jnp.zeros(hz_ref.shape, F32)

        u, bg, cg, v = project(h_refs[0])
        ext = jnp.concatenate([hu_ref[...], u], axis=0)
        sums, width = [], 1
        for w in POOL_WINDOWS:
            while width < w:
                ext = ext + pltpu.roll(ext, width, 0)
                width *= 2
            sums.append(ext)
        s = for_group(sums)[U_HALO:]
        w_here = for_group([jnp.int32(w) for w in POOL_WINDOWS])
        inv_w = for_group([jnp.float32(1.0 / w) for w in POOL_WINDOWS])
        pos = start + lax.broadcasted_iota(jnp.int32, (U_HALO, 1), 0)
        count = jnp.minimum(w_here, pos + 1).astype(F32)
        mean = jnp.concatenate([s[:U_HALO] / count, s[U_HALO:] * inv_w], axis=0)
        z = cg * v
        zext = jnp.concatenate([hz_ref[...], z], axis=0)
        c = (cw0 * pltpu.roll(zext, 2, 0)[Z_HALO:] + cw1 * pltpu.roll(zext, 1, 0)[Z_HALO:]
             + cw2 * z)
        finish(mean - u, bg * c)
        ut_ref[0] = u[tm - U_HALO:]
        zt_ref[0] = z[tm - Z_HALO:]
        hu_ref[...] = u[tm - U_HALO:]
        hz_ref[...] = z[tm - Z_HALO:]

    @pl.when(i >= n_prompt_tiles)
    def _sample():
        nb = dec_batch
        n_prev = sp_ref.shape[0]
        u, bg, cg, v = project(h_refs[-1])
        us_ref[...] = u
        ext = [sp_ref[t] for t in range(n_prev)] + [u[l * nb:(l + 1) * nb] for l in range(dec_seq)]
        d = []
        for l in range(dec_seq):
            t = n_prev + l
            run, sums = ext[t], []
            for j in range(1, max(POOL_WINDOWS)):
                if j in POOL_WINDOWS:
                    sums.append(run)
                run = run + ext[t - j]
            sums.append(run)
            inv_count = for_group([jnp.float32(1.0 / min(w, PAST_LEN + l + 1))
                                   for w in POOL_WINDOWS])
            d.append(for_group(sums) * inv_count - ext[t])
        z = cg * v
        zext = [sc_ref[t] for t in range(sc_ref.shape[0])] + [z[l * nb:(l + 1) * nb]
                                                              for l in range(dec_seq)]
        c = [cw0 * zext[l] + cw1 * zext[l + 1] + cw2 * zext[l + 2] for l in range(dec_seq)]
        finish(jnp.concatenate(d, axis=0), bg * jnp.concatenate(c, axis=0))
        zs_ref[...] = z[tm - zs_ref.shape[0]:]
        ut_ref[0] = u[tm - U_HALO:]
        zt_ref[0] = jnp.zeros(zt_ref.shape[1:], F32)


def _mixin(h, w_in, layer, state_pool_tm, state_conv_tm, w_pool, pool_scale, conv_w,
           to_bf16, *, n_prompt_rows, seq):
    h_parts = _parts(h)
    rows, d = sum(p.shape[0] for p in h_parts), h_parts[0].shape[1]
    assert len(h_parts) == 1 or h_parts[0].shape[0] == n_prompt_rows
    tm = ROW_TILE
    n_groups = len(POOL_WINDOWS)
    pw = state_pool_tm.shape[2]
    cwid = state_conv_tm.shape[2]
    gc = pw // n_groups
    assert w_in.shape[2] == pw + 3 * cwid and cwid == pw
    n_tiles = rows // tm
    n_prompt_tiles = n_prompt_rows // tm
    dec_batch = state_pool_tm.shape[1]
    dec_seq = (rows - n_prompt_rows) // dec_batch
    assert n_tiles == n_prompt_tiles + 1 and dec_seq * dec_batch == tm and seq % tm == 0
    n_conv_prev = state_conv_tm.shape[0]
    assert CAST_STEPS <= n_groups * n_tiles
    cast_rows = [w.shape[1] // CAST_STEPS for w in to_bf16]
    assert all(w.shape[1] % CAST_STEPS == 0 and r % (2 * SUBLANES_F32) == 0
               for w, r in zip(to_bf16, cast_rows))

    def chunk(g, i):
        return jnp.minimum(g * n_tiles + i, CAST_STEPS - 1)

    def w_in_spec(k):
        return pl.BlockSpec((1, d, gc), lambda g, i: (layer, 0, k * n_groups + g))

    kernel = functools.partial(_mixin_kernel, n_h_parts=len(h_parts), n_cast=len(to_bf16),
                               n_prompt_tiles=n_prompt_tiles, tiles_per_seq=seq // tm,
                               dec_batch=dec_batch, dec_seq=dec_seq)
    cast_in_specs = [pl.BlockSpec((1, r, w.shape[2]), lambda g, i: (layer, chunk(g, i), 0))
                     for w, r in zip(to_bf16, cast_rows)]
    cast_out_specs = [pl.BlockSpec((r, w.shape[2]), lambda g, i: (chunk(g, i), 0))
                      for w, r in zip(to_bf16, cast_rows)]
    cast_out_shape = [jax.ShapeDtypeStruct(w.shape[1:], BF16) for w in to_bf16]
    return pl.pallas_call(
        kernel,
        grid=(n_groups, n_tiles),
        in_specs=_part_specs(h_parts, d, lambda g, i: 0, 1) + [
            w_in_spec(0), w_in_spec(1), w_in_spec(2), w_in_spec(3),
            pl.BlockSpec(state_pool_tm.shape[:2] + (gc,), lambda g, i: (0, 0, g)),
            pl.BlockSpec(state_conv_tm.shape[:2] + (gc,), lambda g, i: (0, 0, g)),
            pl.BlockSpec((1, 1, gc, gc), lambda g, i: (layer, g, 0, 0)),
            pl.BlockSpec((1, gc), lambda g, i: (0, g)),
            pl.BlockSpec((conv_w.shape[0], gc), lambda g, i: (0, g)),
        ] + cast_in_specs,
        out_specs=[
            pl.BlockSpec((tm, gc), lambda g, i: (i, g)),
            pl.BlockSpec((tm, gc), lambda g, i: (i, g)),
            pl.BlockSpec((1, U_HALO, gc), lambda g, i: (i, 0, g)),
            pl.BlockSpec((1, Z_HALO, gc), lambda g, i: (i, 0, g)),
            pl.BlockSpec((tm, gc), lambda g, i: (0, g)),
            pl.BlockSpec((n_conv_prev * dec_batch, gc), lambda g, i: (0, g)),
        ] + cast_out_specs,
        out_shape=[
            jax.ShapeDtypeStruct((rows, pw), BF16),
            jax.ShapeDtypeStruct((rows, cwid), BF16),
            jax.ShapeDtypeStruct((n_tiles, U_HALO, pw), F32),
            jax.ShapeDtypeStruct((n_tiles, Z_HALO, cwid), F32),
            jax.ShapeDtypeStruct((tm, pw), F32),
            jax.ShapeDtypeStruct((n_conv_prev * dec_batch, cwid), F32),
        ] + cast_out_shape,
        scratch_shapes=[
            pltpu.VMEM((d, 4 * gc), BF16),
            pltpu.VMEM((U_HALO, gc), F32),
            pltpu.VMEM((Z_HALO, gc), F32),
        ],
        compiler_params=_cparams("arbitrary", "arbitrary"),
        name="mixin",
    )(*h_parts, w_in, w_in, w_in, w_in, state_pool_tm, state_conv_tm, w_pool,
      pool_scale.reshape(1, pw), conv_w, *to_bf16)


def _masked_softmax_rows(s, valid):
    s = jnp.where(valid, s, -jnp.inf)
    m = jnp.max(s, axis=-1, keepdims=True)
    e = jnp.exp(s - m)
    return e * (1.0 / jnp.sum(e, axis=-1, keepdims=True))


def _softmax_rows(s):
    m = jnp.max(s, axis=-1, keepdims=True)
    e = jnp.exp(s - m)
    return e * (1.0 / jnp.sum(e, axis=-1, keepdims=True))


def _attn_prompt_kernel(q_ref, k_ref, v_ref, wo_ref, x_ref, g_ref, xo_ref, ho_ref,
                        o_even_ref, o_odd_ref, *, n_tiles):
    t = pl.program_id(0)
    hd = k_ref.shape[-1]
    scale = 1.0 / math.sqrt(hd)

    def step(o_new_ref, o_old_ref):
        out = []
        for c in range(N_XHEADS):
            cols = slice(c * hd, (c + 1) * hd)
            if o_new_ref is not None:
                s = lax.dot_general(q_ref[:, cols], k_ref[0, 0, c], (((1,), (1,)), ((), ())),
                                    preferred_element_type=F32) * scale
            if o_old_ref is not None:
                out.append(jnp.dot(o_old_ref[...], wo_ref[:, cols], preferred_element_type=F32))
            if o_new_ref is not None:
                p = _softmax_rows(s).astype(BF16)
                o_new_ref[:, cols] = jnp.dot(p, v_ref[0, 0, c],
                                             preferred_element_type=F32).astype(BF16)
        if o_old_ref is not None:
            acc = x_ref[...] + jnp.concatenate(out, axis=1)
            xo_ref[...] = acc
            ho_ref[...] = _rms_rows(acc, g_ref[...]).astype(ho_ref.dtype)

    last_ref = o_odd_ref if (n_tiles - 1) % 2 else o_even_ref
    pl.when(t == 0)(lambda: step(o_even_ref, None))
    pl.when((t > 0) & (t < n_tiles) & (t % 2 == 1))(lambda: step(o_odd_ref, o_even_ref))
    pl.when((t > 0) & (t < n_tiles) & (t % 2 == 0))(lambda: step(o_even_ref, o_odd_ref))
    pl.when(t == n_tiles)(lambda: step(None, last_ref))


def _attn_prompt(q, mk, mv, layer, w_o, x, norm_g, *, batch, seq):
    d = q.shape[1]
    kv_block = (1, 1) + mk.shape[2:]
    tq = ROW_TILE
    tps = seq // tq
    n_tiles = batch * tps

    def attended(t):
        return jnp.minimum(t, n_tiles - 1)

    def projected(t):
        return jnp.maximum(t - 1, 0)

    out_spec = pl.BlockSpec((tq, d), lambda t: (projected(t), 0))
    return pl.pallas_call(
        functools.partial(_attn_prompt_kernel, n_tiles=n_tiles),
        grid=(n_tiles + 1,),
        in_specs=[pl.BlockSpec((tq, d), lambda t: (attended(t), 0)),
                  pl.BlockSpec(kv_block, lambda t: (layer, attended(t) // tps, 0, 0, 0)),
                  pl.BlockSpec(kv_block, lambda t: (layer, attended(t) // tps, 0, 0, 0)),
                  pl.BlockSpec((d, d), lambda t: (0, 0), pipeline_mode=pl.Buffered(1)),
                  out_spec,
                  pl.BlockSpec((1, d), lambda t: (0, 0))],
        out_specs=[out_spec, out_spec],
        out_shape=[jax.ShapeDtypeStruct((n_tiles * tq, d), F32),
                   jax.ShapeDtypeStruct((n_tiles * tq, d), BF16)],
        scratch_shapes=[pltpu.VMEM((tq, d), BF16), pltpu.VMEM((tq, d), BF16)],
        compiler_params=_cparams("arbitrary"),
        name="attn_prompt",
    )(q, mk, mv, w_o, x, norm_g.reshape(1, d))


def _sample_attn_operands(q, k, v):
    lq, d = q.shape
    n_mem, n_heads, hd = k.shape
    assert lq <= SUBLANES_F32
    q8 = jnp.concatenate([q, jnp.zeros((SUBLANES_F32 - lq, d), F32)], axis=0)
    qh = jnp.concatenate([q8[:, h * hd:(h + 1) * hd] for h in range(n_heads)], axis=0)
    k2 = k.reshape(n_mem * n_heads, hd).astype(BF16)
    v2 = v.reshape(n_mem * n_heads, hd).astype(BF16)
    return qh.astype(BF16), k2, v2


def _sample_attn_probs(qh, k2, n_heads):
    n_rows, hd = qh.shape
    n_keys = k2.shape[0]
    head_of_row = lax.broadcasted_iota(jnp.int32, (n_rows, n_keys), 0) // SUBLANES_F32
    head_of_key = lax.broadcasted_iota(jnp.int32, (n_rows, n_keys), 1) % n_heads
    s = lax.dot_general(qh, k2, (((1,), (1,)), ((), ())),
                        preferred_element_type=F32) * (1.0 / math.sqrt(hd))
    return _masked_softmax_rows(s, head_of_row == head_of_key).astype(BF16)


def _sample_attn_output(p, v2, lq, n_heads):
    o = jnp.dot(p, v2, preferred_element_type=F32)
    return jnp.concatenate(
        [o[h * SUBLANES_F32:h * SUBLANES_F32 + lq] for h in range(n_heads)], axis=1)


def _mlp_kernel(*refs, emit_x, with_attn, n_chunks):
    refs = list(refs)
    h_ref, x_ref, wu_ref, wd_ref, g_ref = (refs.pop(0) for _ in range(5))
    q_ref, k_ref, v_ref = (refs.pop(0) for _ in range(3)) if with_attn else (None,) * 3
    xo_ref = refs.pop(0) if emit_x else None
    ho_ref = refs.pop(0)
    os_ref = refs.pop(0) if with_attn else None
    acc_ref, a_even_ref, a_odd_ref = refs
    j = pl.program_id(1)
    tf, d = wd_ref.shape
    pieces = MLP_PIPELINE_PIECES

    def step(a_new_ref, a_old_ref):
        if with_attn and a_new_ref is not None:
            n_heads = k_ref.shape[3]
            qh, k2, v2 = _sample_attn_operands(q_ref[0], k_ref[0, 0], v_ref[0, 0])
        for c in range(pieces):
            if a_new_ref is not None:
                fc = slice(c * tf // pieces, (c + 1) * tf // pieces)
                t = jnp.dot(h_ref[...], wu_ref[:, fc], preferred_element_type=F32)
                a_new_ref[:, fc] = jnp.square(jnp.maximum(t, 0.0)).astype(BF16)
            if a_old_ref is not None:
                dc = slice(c * d // pieces, (c + 1) * d // pieces)
                acc_ref[:, dc] += jnp.dot(a_old_ref[...], wd_ref[:, dc],
                                          preferred_element_type=F32)
            if with_attn and a_new_ref is not None:
                if c == 0:
                    p = _sample_attn_probs(qh, k2, n_heads)
                if c == pieces - 1:
                    os_ref[0] = _sample_attn_output(p, v2, q_ref.shape[1], n_heads)

    @pl.when(j == 0)
    def _():
        acc_ref[...] = x_ref[...]
        step(a_even_ref, None)

    pl.when((j > 0) & (j < n_chunks) & (j % 2 == 1))(lambda: step(a_odd_ref, a_even_ref))
    pl.when((j > 0) & (j < n_chunks) & (j % 2 == 0))(lambda: step(a_even_ref, a_odd_ref))

    @pl.when(j == n_chunks)
    def _():
        step(None, a_odd_ref if (n_chunks - 1) % 2 else a_even_ref)
        x = acc_ref[...]
        if emit_x:
            xo_ref[...] = x
        ho_ref[...] = _rms_rows(x, g_ref[...]).astype(ho_ref.dtype)


MLP_PIPELINE_PIECES = 2


MLP_HIDDEN_TILE = 1024


def _mlp(h, x, w_up, w_down, norm_g, *, norm_dtype=BF16, emit_x=True, sample_attn=None):
    m, d = x.shape
    f = w_up.shape[1]
    tf = MLP_HIDDEN_TILE
    n_tiles, n_chunks = m // ROW_TILE, f // tf
    row_spec = pl.BlockSpec((ROW_TILE, d), lambda i, j: (i, 0))

    def up_chunk(j):
        return jnp.minimum(j, n_chunks - 1)

    def down_chunk(j):
        return jnp.maximum(j - 1, 0)

    in_specs = [row_spec, row_spec,
                pl.BlockSpec((d, tf), lambda i, j: (0, up_chunk(j))),
                pl.BlockSpec((tf, d), lambda i, j: (down_chunk(j), 0)),
                pl.BlockSpec((1, d), lambda i, j: (0, 0))]
    args = [h, x, w_up, w_down, norm_g.reshape(1, d)]
    out_specs, out_shape = [], []
    if emit_x:
        out_specs.append(row_spec)
        out_shape.append(jax.ShapeDtypeStruct((m, d), F32))
    out_specs.append(row_spec)
    out_shape.append(jax.ShapeDtypeStruct((m, d), norm_dtype))
    if sample_attn is not None:
        q, cache_k, cache_v, layer = sample_attn
        nb, lq, _ = q.shape
        assert nb == n_tiles * n_chunks

        def seq(i, j):
            return i * n_chunks + up_chunk(j)

        kv_block = (1, 1) + cache_k.shape[2:]
        in_specs += [pl.BlockSpec((1, lq, d), lambda i, j: (seq(i, j), 0, 0)),
                     pl.BlockSpec(kv_block, lambda i, j: (layer, seq(i, j), 0, 0, 0)),
                     pl.BlockSpec(kv_block, lambda i, j: (layer, seq(i, j), 0, 0, 0))]
        args += [q, cache_k, cache_v]
        out_specs.append(pl.BlockSpec((1, lq, d), lambda i, j: (seq(i, j), 0, 0)))
        out_shape.append(jax.ShapeDtypeStruct((nb, lq, d), F32))
    return pl.pallas_call(
        functools.partial(_mlp_kernel, emit_x=emit_x, with_attn=sample_attn is not None,
                          n_chunks=n_chunks),
        grid=(n_tiles, n_chunks + 1),
        in_specs=in_specs,
        out_specs=out_specs,
        out_shape=out_shape,
        scratch_shapes=[pltpu.VMEM((ROW_TILE, d), F32),
                        pltpu.VMEM((ROW_TILE, tf), BF16),
                        pltpu.VMEM((ROW_TILE, tf), BF16)],
        compiler_params=_cparams("arbitrary", "arbitrary"),
        name="mlp",
    )(*args)


def kernel(x_prompt, x_sample, mem_prompt, cache_mem_k, cache_mem_v, state_pool, state_conv,
           norm_mix, w_in, w_pool, pool_scale, conv_w, w_out, norm_attn, norm_mem, w_q, w_k, w_v,
           w_o, norm_mlp, w_up, w_down, norm_final):
    batch, seq, d = x_prompt.shape
    nb, lq, _ = x_sample.shape
    depth = w_in.shape[0]
    pool_buf, pw = state_pool.shape[2], state_pool.shape[3]
    conv_buf, cwid = state_conv.shape[2], state_conv.shape[3]
    n_prompt_rows = batch * seq
    tiles_per_seq = seq // ROW_TILE
    assert nb * lq == ROW_TILE and pool_buf < U_HALO and conv_buf == CONV_K - 1 <= Z_HALO
    assert cache_mem_k.shape[3] == N_XHEADS

    mem_k, mem_k_heads = _mem_proj(mem_prompt, norm_mem, w_k)
    mem_v, mem_v_heads = _mem_proj(mem_prompt, norm_mem, w_v)

    x = [x_prompt.reshape(n_prompt_rows, d), x_sample.transpose(1, 0, 2).reshape(lq * nb, d)]
    h = _rms(x, norm_mix[0])
    pool_p, conv_p, pool_s, conv_s = [], [], [], []
    for l in range(depth):
        sp_tm = state_pool[l].transpose(1, 0, 2)
        sc_tm = state_conv[l].transpose(1, 0, 2)
        ya, yb, u_tail, z_tail, u_s, z_s, w_out_b, w_q_b, w_o_b, w_up_b, w_down_b = _mixin(
            h, w_in, l, sp_tm, sc_tm, w_pool, pool_scale[l], conv_w[l],
            [w_out, w_q, w_o, w_up, w_down], n_prompt_rows=n_prompt_rows, seq=seq)
        x, q = _mm([ya, yb], w_out_b, res=x, norm_g=norm_attn[l], w2=w_q_b)
        last_tiles = slice(tiles_per_seq - 1, batch * tiles_per_seq, tiles_per_seq)
        pool_p.append(u_tail[last_tiles, U_HALO - pool_buf:])
        conv_p.append(z_tail[last_tiles, Z_HALO - conv_buf:])
        pool_s.append(jnp.concatenate([sp_tm, u_s.reshape(lq, nb, pw)], axis=0)[-pool_buf:])
        conv_s.append(z_s.reshape(conv_buf, nb, cwid).transpose(1, 0, 2))
        x_p, h_p = _attn_prompt(q, mem_k_heads, mem_v_heads, l, w_o_b, x, norm_mlp[l],
                                batch=batch, seq=seq)
        q_s = q[n_prompt_rows:].astype(F32).reshape(lq, nb, d).transpose(1, 0, 2)
        last = l == depth - 1
        g_next = norm_final if last else norm_mix[l + 1]
        mlp_opts = dict(norm_dtype=F32 if last else BF16, emit_x=not last)
        *xh_p, o_s = _mlp(h_p, x_p, w_up_b, w_down_b, g_next, **mlp_opts,
                          sample_attn=(q_s, cache_mem_k, cache_mem_v, l))
        o_s = o_s.transpose(1, 0, 2).reshape(lq * nb, d).astype(BF16)
        x_s, h_s = _mm([o_s], w_o_b, res=x[n_prompt_rows:], norm_g=norm_mlp[l])
        xh_s = _mlp(h_s, x_s, w_up_b, w_down_b, g_next, **mlp_opts)
        if last:
            (y_p,), (y_s,) = xh_p, xh_s
        else:
            (x_p, h_p), (x_s, h_s) = xh_p, xh_s
            x, h = [x_p, x_s], [h_p, h_s]

    y_prompt = y_p.reshape(batch, seq, d)
    y_sample = y_s.reshape(lq, nb, d).transpose(1, 0, 2)
    return (y_prompt, y_sample, jnp.stack(pool_p), jnp.stack(conv_p), mem_k, mem_v,
            jnp.stack(pool_s).transpose(0, 2, 1, 3), jnp.stack(conv_s))
```

```python
import functools
import math

import jax
import jax.numpy as jnp
from jax import lax
from jax.experimental import pallas as pl
from jax.experimental.pallas import tpu as pltpu

F32 = jnp.float32
BF16 = jnp.bfloat16

PAST_LEN = 16384
POOL_WINDOWS = (2, 4, 8, 16)
CONV_K = 3
N_XHEADS = 4
EPS = 1e-6

V7X_VMEM_BYTES = 64 * 1024 * 1024
VMEM_LIMIT_BYTES = V7X_VMEM_BYTES - 8 * 1024 * 1024
SUBLANES_F32 = 8
LANES = 128

ROW_TILE = 512


def _cparams(*sem):
    return pltpu.CompilerParams(dimension_semantics=sem, vmem_limit_bytes=VMEM_LIMIT_BYTES)


def _rms_rows(x, g):
    ms = jnp.mean(x * x, axis=-1, keepdims=True)
    return x * lax.rsqrt(ms + EPS) * g


def _parts(x):
    return list(x) if isinstance(x, (list, tuple)) else [x]


def _part_specs(parts, cols, col_index, row_axis):
    specs, start = [], 0
    for p in parts:
        n = p.shape[0] // ROW_TILE
        assert p.shape[0] % ROW_TILE == 0

        def index_map(*ids, start=start, n=n):
            i = ids[row_axis]
            return (jnp.clip(i - start, 0, n - 1), col_index(*ids))

        mode = {"pipeline_mode": pl.Buffered(1)} if n == 1 and len(parts) > 1 else {}
        specs.append(pl.BlockSpec((ROW_TILE, cols), index_map, **mode))
        start += n
    return specs


def _read_parts(refs, i, first_part_tiles):
    if len(refs) == 1:
        return refs[0][...]
    return jnp.where(i < first_part_tiles, refs[0][...], refs[1][...])


def _rms_kernel(*refs, n_parts, first_part_tiles):
    x_refs, g_ref, o_ref = refs[:n_parts], refs[n_parts], refs[n_parts + 1]
    x = _read_parts(x_refs, pl.program_id(0), first_part_tiles)
    o_ref[...] = _rms_rows(x, g_ref[...]).astype(o_ref.dtype)


def _rms(x, g, out_dtype=BF16):
    parts = _parts(x)
    rows = sum(p.shape[0] for p in parts)
    d = parts[0].shape[1]
    return pl.pallas_call(
        functools.partial(_rms_kernel, n_parts=len(parts),
                          first_part_tiles=parts[0].shape[0] // ROW_TILE),
        grid=(rows // ROW_TILE,),
        in_specs=_part_specs(parts, d, lambda i: 0, 0) + [pl.BlockSpec((1, d), lambda i: (0, 0))],
        out_specs=pl.BlockSpec((ROW_TILE, d), lambda i: (i, 0)),
        out_shape=jax.ShapeDtypeStruct((rows, d), out_dtype),
        compiler_params=_cparams("parallel"),
        name="rms",
    )(*parts, g.reshape(1, d))


def _mm_kernel(*refs, kpart_sizes, n_res, chained, first_part_tiles):
    refs = list(refs)
    a_refs = [[refs.pop(0) for _ in range(n)] for n in kpart_sizes]
    w_ref = refs.pop(0)
    res_refs = [refs.pop(0) for _ in range(n_res)]
    g_ref = refs.pop(0)
    w2_ref = refs.pop(0) if chained else None
    o_ref, h_ref = refs
    i = pl.program_id(0)

    acc, k0 = None, 0
    for part_refs in a_refs:
        a = _read_parts(part_refs, i, first_part_tiles)
        t = jnp.dot(a, w_ref[k0:k0 + a.shape[1], :], preferred_element_type=F32)
        acc = t if acc is None else acc + t
        k0 += a.shape[1]
    acc = acc + _read_parts(res_refs, i, first_part_tiles)
    o_ref[...] = acc
    h = _rms_rows(acc, g_ref[...]).astype(BF16)
    if chained:
        h_ref[...] = jnp.dot(h, w2_ref[...], preferred_element_type=F32).astype(h_ref.dtype)
    else:
        h_ref[...] = h


def _mm(a_kparts, w, *, res, norm_g, w2=None):
    a_kparts = [_parts(p) for p in a_kparts]
    res_parts = _parts(res)
    m = sum(p.shape[0] for p in a_kparts[0])
    k, n = w.shape
    assert sum(ps[0].shape[1] for ps in a_kparts) == k
    split = {ps[0].shape[0] // ROW_TILE for ps in a_kparts + [res_parts] if len(ps) == 2}
    assert len(split) <= 1
    first_part_tiles = split.pop() if split else m // ROW_TILE
    resident = dict(pipeline_mode=pl.Buffered(1))
    in_specs, args = [], []
    for ps in a_kparts:
        in_specs += _part_specs(ps, ps[0].shape[1], lambda i: 0, 0)
        args += ps
    in_specs.append(pl.BlockSpec((k, n), lambda i: (0, 0), **resident))
    in_specs += _part_specs(res_parts, n, lambda i: 0, 0)
    in_specs.append(pl.BlockSpec((1, n), lambda i: (0, 0)))
    args += [w] + res_parts + [norm_g.reshape(1, n)]
    n2 = n
    if w2 is not None:
        n2 = w2.shape[1]
        in_specs.append(pl.BlockSpec((n, n2), lambda i: (0, 0), **resident))
        args.append(w2)
    return pl.pallas_call(
        functools.partial(_mm_kernel, kpart_sizes=tuple(len(ps) for ps in a_kparts),
                          n_res=len(res_parts), chained=w2 is not None,
                          first_part_tiles=first_part_tiles),
        grid=(m // ROW_TILE,),
        in_specs=in_specs,
        out_specs=[pl.BlockSpec((ROW_TILE, n), lambda i: (i, 0)),
                   pl.BlockSpec((ROW_TILE, n2), lambda i: (i, 0))],
        out_shape=[jax.ShapeDtypeStruct((m, n), F32), jax.ShapeDtypeStruct((m, n2), BF16)],
        compiler_params=_cparams("parallel"),
        name="mm",
    )(*args)


MEM_ROW_TILE = 256


def _mem_proj_kernel(m_ref, g_ref, w_ref, o_ref, ob_ref, wb_ref):
    @pl.when(pl.program_id(1) == 0)
    def _():
        wb_ref[...] = w_ref[0].astype(BF16)

    h = _rms_rows(m_ref[...], g_ref[0]).astype(BF16)
    kv = jnp.dot(h, wb_ref[...], preferred_element_type=F32)
    hd = o_ref.shape[-1]
    for head in range(N_XHEADS):
        o_ref[0, 0, :, head, :] = kv[:, head * hd:(head + 1) * hd]
        ob_ref[0, 0, head] = kv[:, head * hd:(head + 1) * hd].astype(BF16)


def _mem_proj(mem, norm_mem, w):
    batch, n_mem, d = mem.shape
    depth = w.shape[0]
    tm = MEM_ROW_TILE
    assert n_mem % tm == 0
    tiles_per_seq = n_mem // tm
    hd = d // N_XHEADS
    return pl.pallas_call(
        _mem_proj_kernel,
        grid=(depth, batch * tiles_per_seq),
        in_specs=[pl.BlockSpec((tm, d), lambda l, i: (i, 0)),
                  pl.BlockSpec((1, 1, d), lambda l, i: (l, 0, 0)),
                  pl.BlockSpec((1, d, d), lambda l, i: (l, 0, 0))],
        out_specs=[pl.BlockSpec((1, 1, tm, N_XHEADS, hd),
                                lambda l, i: (l, i // tiles_per_seq, i % tiles_per_seq, 0, 0)),
                   pl.BlockSpec((1, 1, N_XHEADS, tm, hd),
                                lambda l, i: (l, i // tiles_per_seq, 0, i % tiles_per_seq, 0))],
        out_shape=[jax.ShapeDtypeStruct((depth, batch, n_mem, N_XHEADS, hd), F32),
                   jax.ShapeDtypeStruct((depth, batch, N_XHEADS, n_mem, hd), BF16)],
        scratch_shapes=[pltpu.VMEM((d, d), BF16)],
        compiler_params=_cparams("arbitrary", "arbitrary"),
        name="mem_proj",
    )(mem.reshape(batch * n_mem, d), norm_mem.reshape(depth, 1, d), w)


U_HALO = 16
Z_HALO = 8
CAST_STEPS = 64


def _mixin_kernel(*refs, n_h_parts, n_cast, n_prompt_tiles, tiles_per_seq, dec_batch, dec_seq):
    refs = list(refs)
    h_refs = [refs.pop(0) for _ in range(n_h_parts)]
    wu_ref, wb_ref, wc_ref, wv_ref, sp_ref, sc_ref, wp_ref, ps_ref, cw_ref = (
        refs.pop(0) for _ in range(9))
    cast_src = [refs.pop(0) for _ in range(n_cast)]
    ya_ref, yb_ref, ut_ref, zt_ref, us_ref, zs_ref = (refs.pop(0) for _ in range(6))
    cast_dst = [refs.pop(0) for _ in range(n_cast)]
    wcat_ref, hu_ref, hz_ref = refs
    tm = ROW_TILE
    g, i = pl.program_id(0), pl.program_id(1)
    gc = hu_ref.shape[1]
    cw0, cw1, cw2 = cw_ref[0:1, :], cw_ref[1:2, :], cw_ref[2:3, :]

    def for_group(values):
        out = values[-1]
        for gg in range(len(values) - 2, -1, -1):
            out = jnp.where(g == gg, values[gg], out)
        return out

    def project(h_ref):
        for src_ref, dst_ref in zip(cast_src, cast_dst):
            dst_ref[...] = src_ref[0].astype(BF16)
        pj = jnp.dot(h_ref[...], wcat_ref[...], preferred_element_type=F32)
        return pj[:, 0:gc], pj[:, gc:2 * gc], pj[:, 2 * gc:3 * gc], pj[:, 3 * gc:4 * gc]

    def finish(d, yb):
        ya = jnp.dot(d.astype(BF16), wp_ref[0, 0].astype(BF16), preferred_element_type=F32)
        ya_ref[...] = (ya * ps_ref[...]).astype(BF16)
        yb_ref[...] = yb.astype(BF16)

    @pl.when(i == 0)
    def _():
        for k, w_ref in enumerate((wu_ref, wb_ref, wc_ref, wv_ref)):
            wcat_ref[:, k * gc:(k + 1) * gc] = w_ref[0].astype(BF16)

    @pl.when(i < n_prompt_tiles)
    def _prompt():
        start = (i % tiles_per_seq) * tm

        @pl.when(start == 0)
        def _():
            hu_ref[...] = jnp.zeros(hu_ref.shape, F32)
            hz_ref[...] = jnp.zeros(hz_ref.shape, F32)

        u, bg, cg, v = project(h_refs[0])
        ext = jnp.concatenate([hu_ref[...], u], axis=0)
        sums, width = [], 1
        for w in POOL_WINDOWS:
            while width < w:
                ext = ext + pltpu.roll(ext, width, 0)
                width *= 2
            sums.append(ext)
        s = for_group(sums)[U_HALO:]
        w_here = for_group([jnp.int32(w) for w in POOL_WINDOWS])
        inv_w = for_group([jnp.float32(1.0 / w) for w in POOL_WINDOWS])
        pos = start + lax.broadcasted_iota(jnp.int32, (U_HALO, 1), 0)
        count = jnp.minimum(w_here, pos + 1).astype(F32)
        mean = jnp.concatenate([s[:U_HALO] / count, s[U_HALO:] * inv_w], axis=0)
        z = cg * v
        zext = jnp.concatenate([hz_ref[...], z], axis=0)
        c = (cw0 * pltpu.roll(zext, 2, 0)[Z_HALO:] + cw1 * pltpu.roll(zext, 1, 0)[Z_HALO:]
             + cw2 * z)
        finish(mean - u, bg * c)
        ut_ref[0] = u[tm - U_HALO:]
        zt_ref[0] = z[tm - Z_HALO:]
        hu_ref[...] = u[tm - U_HALO:]
        hz_ref[...] = z[tm - Z_HALO:]

    @pl.when(i >= n_prompt_tiles)
    def _sample():
        nb = dec_batch
        n_prev = sp_ref.shape[0]
        u, bg, cg, v = project(h_refs[-1])
        us_ref[...] = u
        ext = [sp_ref[t] for t in range(n_prev)] + [u[l * nb:(l + 1) * nb] for l in range(dec_seq)]
        d = []
        for l in range(dec_seq):
            t = n_prev + l
            run, sums = ext[t], []
            for j in range(1, max(POOL_WINDOWS)):
                if j in POOL_WINDOWS:
                    sums.append(run)
                run = run + ext[t - j]
            sums.append(run)
            inv_count = for_group([jnp.float32(1.0 / min(w, PAST_LEN + l + 1))
                                   for w in POOL_WINDOWS])
            d.append(for_group(sums) * inv_count - ext[t])
        z = cg * v
        zext = [sc_ref[t] for t in range(sc_ref.shape[0])] + [z[l * nb:(l + 1) * nb]
                                                              for l in range(dec_seq)]
        c = [cw0 * zext[l] + cw1 * zext[l + 1] + cw2 * zext[l + 2] for l in range(dec_seq)]
        finish(jnp.concatenate(d, axis=0), bg * jnp.concatenate(c, axis=0))
        zs_ref[...] = z[tm - zs_ref.shape[0]:]
        ut_ref[0] = u[tm - U_HALO:]
        zt_ref[0] = jnp.zeros(zt_ref.shape[1:], F32)


def _mixin(h, w_in, layer, state_pool_tm, state_conv_tm, w_pool, pool_scale, conv_w,
           to_bf16, *, n_prompt_rows, seq):
    h_parts = _parts(h)
    rows, d = sum(p.shape[0] for p in h_parts), h_parts[0].shape[1]
    assert len(h_parts) == 1 or h_parts[0].shape[0] == n_prompt_rows
    tm = ROW_TILE
    n_groups = len(POOL_WINDOWS)
    pw = state_pool_tm.shape[2]
    cwid = state_conv_tm.shape[2]
    gc = pw // n_groups
    assert w_in.shape[2] == pw + 3 * cwid and cwid == pw
    n_tiles = rows // tm
    n_prompt_tiles = n_prompt_rows // tm
    dec_batch = state_pool_tm.shape[1]
    dec_seq = (rows - n_prompt_rows) // dec_batch
    assert n_tiles == n_prompt_tiles + 1 and dec_seq * dec_batch == tm and seq % tm == 0
    n_conv_prev = state_conv_tm.shape[0]
    assert CAST_STEPS <= n_groups * n_tiles
    cast_rows = [w.shape[1] // CAST_STEPS for w in to_bf16]
    assert all(w.shape[1] % CAST_STEPS == 0 and r % (2 * SUBLANES_F32) == 0
               for w, r in zip(to_bf16, cast_rows))

    def chunk(g, i):
        return jnp.minimum(g * n_tiles + i, CAST_STEPS - 1)

    def w_in_spec(k):
        return pl.BlockSpec((1, d, gc), lambda g, i: (layer, 0, k * n_groups + g))

    kernel = functools.partial(_mixin_kernel, n_h_parts=len(h_parts), n_cast=len(to_bf16),
                               n_prompt_tiles=n_prompt_tiles, tiles_per_seq=seq // tm,
                               dec_batch=dec_batch, dec_seq=dec_seq)
    cast_in_specs = [pl.BlockSpec((1, r, w.shape[2]), lambda g, i: (layer, chunk(g, i), 0))
                     for w, r in zip(to_bf16, cast_rows)]
    cast_out_specs = [pl.BlockSpec((r, w.shape[2]), lambda g, i: (chunk(g, i), 0))
                      for w, r in zip(to_bf16, cast_rows)]
    cast_out_shape = [jax.ShapeDtypeStruct(w.shape[1:], BF16) for w in to_bf16]
    return pl.pallas_call(
        kernel,
        grid=(n_groups, n_tiles),
        in_specs=_part_specs(h_parts, d, lambda g, i: 0, 1) + [
            w_in_spec(0), w_in_spec(1), w_in_spec(2), w_in_spec(3),
            pl.BlockSpec(state_pool_tm.shape[:2] + (gc,), lambda g, i: (0, 0, g)),
            pl.BlockSpec(state_conv_tm.shape[:2] + (gc,), lambda g, i: (0, 0, g)),
            pl.BlockSpec((1, 1, gc, gc), lambda g, i: (layer, g, 0, 0)),
            pl.BlockSpec((1, gc), lambda g, i: (0, g)),
            pl.BlockSpec((conv_w.shape[0], gc), lambda g, i: (0, g)),
        ] + cast_in_specs,
        out_specs=[
            pl.BlockSpec((tm, gc), lambda g, i: (i, g)),
            pl.BlockSpec((tm, gc), lambda g, i: (i, g)),
            pl.BlockSpec((1, U_HALO, gc), lambda g, i: (i, 0, g)),
            pl.BlockSpec((1, Z_HALO, gc), lambda g, i: (i, 0, g)),
            pl.BlockSpec((tm, gc), lambda g, i: (0, g)),
            pl.BlockSpec((n_conv_prev * dec_batch, gc), lambda g, i: (0, g)),
        ] + cast_out_specs,
        out_shape=[
            jax.ShapeDtypeStruct((rows, pw), BF16),
            jax.ShapeDtypeStruct((rows, cwid), BF16),
            jax.ShapeDtypeStruct((n_tiles, U_HALO, pw), F32),
            jax.ShapeDtypeStruct((n_tiles, Z_HALO, cwid), F32),
            jax.ShapeDtypeStruct((tm, pw), F32),
            jax.ShapeDtypeStruct((n_conv_prev * dec_batch, cwid), F32),
        ] + cast_out_shape,
        scratch_shapes=[
            pltpu.VMEM((d, 4 * gc), BF16),
            pltpu.VMEM((U_HALO, gc), F32),
            pltpu.VMEM((Z_HALO, gc), F32),
        ],
        compiler_params=_cparams("arbitrary", "arbitrary"),
        name="mixin",
    )(*h_parts, w_in, w_in, w_in, w_in, state_pool_tm, state_conv_tm, w_pool,
      pool_scale.reshape(1, pw), conv_w, *to_bf16)


def _masked_softmax_rows(s, valid):
    s = jnp.where(valid, s, -jnp.inf)
    m = jnp.max(s, axis=-1, keepdims=True)
    e = jnp.exp(s - m)
    return e * (1.0 / jnp.sum(e, axis=-1, keepdims=True))


def _softmax_rows(s):
    m = jnp.max(s, axis=-1, keepdims=True)
    e = jnp.exp(s - m)
    return e * (1.0 / jnp.sum(e, axis=-1, keepdims=True))


def _attn_prompt_kernel(q_ref, k_ref, v_ref, wo_ref, x_ref, g_ref, xo_ref, ho_ref,
                        o_even_ref, o_odd_ref, *, n_tiles):
    t = pl.program_id(0)
    hd = k_ref.shape[-1]
    scale = 1.0 / math.sqrt(hd)

    def step(o_new_ref, o_old_ref):
        out = []
        for c in range(N_XHEADS):
            cols = slice(c * hd, (c + 1) * hd)
            if o_new_ref is not None:
                s = lax.dot_general(q_ref[:, cols], k_ref[0, 0, c], (((1,), (1,)), ((), ())),
                                    preferred_element_type=F32) * scale
            if o_old_ref is not None:
                out.append(jnp.dot(o_old_ref[...], wo_ref[:, cols], preferred_element_type=F32))
            if o_new_ref is not None:
                p = _softmax_rows(s).astype(BF16)
                o_new_ref[:, cols] = jnp.dot(p, v_ref[0, 0, c],
                                             preferred_element_type=F32).astype(BF16)
        if o_old_ref is not None:
            acc = x_ref[...] + jnp.concatenate(out, axis=1)
            xo_ref[...] = acc
            ho_ref[...] = _rms_rows(acc, g_ref[...]).astype(ho_ref.dtype)

    last_ref = o_odd_ref if (n_tiles - 1) % 2 else o_even_ref
    pl.when(t == 0)(lambda: step(o_even_ref, None))
    pl.when((t > 0) & (t < n_tiles) & (t % 2 == 1))(lambda: step(o_odd_ref, o_even_ref))
    pl.when((t > 0) & (t < n_tiles) & (t % 2 == 0))(lambda: step(o_even_ref, o_odd_ref))
    pl.when(t == n_tiles)(lambda: step(None, last_ref))


def _attn_prompt(q, mk, mv, layer, w_o, x, norm_g, *, batch, seq):
    d = q.shape[1]
    kv_block = (1, 1) + mk.shape[2:]
    tq = ROW_TILE
    tps = seq // tq
    n_tiles = batch * tps

    def attended(t):
        return jnp.minimum(t, n_tiles - 1)

    def projected(t):
        return jnp.maximum(t - 1, 0)

    out_spec = pl.BlockSpec((tq, d), lambda t: (projected(t), 0))
    return pl.pallas_call(
        functools.partial(_attn_prompt_kernel, n_tiles=n_tiles),
        grid=(n_tiles + 1,),
        in_specs=[pl.BlockSpec((tq, d), lambda t: (attended(t), 0)),
                  pl.BlockSpec(kv_block, lambda t: (layer, attended(t) // tps, 0, 0, 0)),
                  pl.BlockSpec(kv_block, lambda t: (layer, attended(t) // tps, 0, 0, 0)),
                  pl.BlockSpec((d, d), lambda t: (0, 0), pipeline_mode=pl.Buffered(1)),
                  out_spec,
                  pl.BlockSpec((1, d), lambda t: (0, 0))],
        out_specs=[out_spec, out_spec],
        out_shape=[jax.ShapeDtypeStruct((n_tiles * tq, d), F32),
                   jax.ShapeDtypeStruct((n_tiles * tq, d), BF16)],
        scratch_shapes=[pltpu.VMEM((tq, d), BF16), pltpu.VMEM((tq, d), BF16)],
        compiler_params=_cparams("arbitrary"),
        name="attn_prompt",
    )(q, mk, mv, w_o, x, norm_g.reshape(1, d))


def _sample_attn_operands(q, k, v):
    lq, d = q.shape
    n_mem, n_heads, hd = k.shape
    assert lq <= SUBLANES_F32
    q8 = jnp.concatenate([q, jnp.zeros((SUBLANES_F32 - lq, d), F32)], axis=0)
    qh = jnp.concatenate([q8[:, h * hd:(h + 1) * hd] for h in range(n_heads)], axis=0)
    k2 = k.reshape(n_mem * n_heads, hd).astype(BF16)
    v2 = v.reshape(n_mem * n_heads, hd).astype(BF16)
    return qh.astype(BF16), k2, v2


def _sample_attn_probs(qh, k2, n_heads):
    n_rows, hd = qh.shape
    n_keys = k2.shape[0]
    head_of_row = lax.broadcasted_iota(jnp.int32, (n_rows, n_keys), 0) // SUBLANES_F32
    head_of_key = lax.broadcasted_iota(jnp.int32, (n_rows, n_keys), 1) % n_heads
    s = lax.dot_general(qh, k2, (((1,), (1,)), ((), ())),
                        preferred_element_type=F32) * (1.0 / math.sqrt(hd))
    return _masked_softmax_rows(s, head_of_row == head_of_key).astype(BF16)


def _sample_attn_output(p, v2, lq, n_heads):
    o = jnp.dot(p, v2, preferred_element_type=F32)
    return jnp.concatenate(
        [o[h * SUBLANES_F32:h * SUBLANES_F32 + lq] for h in range(n_heads)], axis=1)


def _mlp_kernel(*refs, emit_x, with_attn, with_proj):
    refs = list(refs)
    h_ref, x_ref, wu_ref, wd_ref, g_ref = (refs.pop(0) for _ in range(5))
    wo_ref, g0_ref = (refs.pop(0) for _ in range(2)) if with_proj else (None,) * 2
    q_ref, k_ref, v_ref = (refs.pop(0) for _ in range(3)) if with_attn else (None,) * 3
    xo_ref = refs.pop(0) if emit_x else None
    ho_ref = refs.pop(0)
    os_ref = refs.pop(0) if with_attn else None
    acc_ref = refs.pop(0)
    hs_ref = refs.pop(0) if with_proj else None
    j = pl.program_id(1)

    @pl.when(j == 0)
    def _():
        if with_proj:
            x = x_ref[...] + jnp.dot(h_ref[...], wo_ref[...], preferred_element_type=F32)
            acc_ref[...] = x
            hs_ref[...] = _rms_rows(x, g0_ref[...]).astype(BF16)
        else:
            acc_ref[...] = x_ref[...]

    mlp_in_ref = hs_ref if with_proj else h_ref
    a = jnp.dot(mlp_in_ref[...], wu_ref[...], preferred_element_type=F32)
    if with_attn:
        n_heads = k_ref.shape[3]
        qh, k2, v2 = _sample_attn_operands(q_ref[0], k_ref[0, 0], v_ref[0, 0])
        p = _sample_attn_probs(qh, k2, n_heads)
    a = jnp.square(jnp.maximum(a, 0.0)).astype(BF16)
    acc_ref[...] += jnp.dot(a, wd_ref[...], preferred_element_type=F32)
    if with_attn:
        os_ref[0] = _sample_attn_output(p, v2, q_ref.shape[1], n_heads)

    @pl.when(j == pl.num_programs(1) - 1)
    def _():
        x = acc_ref[...]
        if emit_x:
            xo_ref[...] = x
        ho_ref[...] = _rms_rows(x, g_ref[...]).astype(ho_ref.dtype)


MLP_HIDDEN_TILE = 1024


def _mlp(h, x, w_up, w_down, norm_g, *, norm_dtype=BF16, emit_x=True, sample_attn=None,
         attn_proj=None):
    m, d = x.shape
    f = w_up.shape[1]
    tf = MLP_HIDDEN_TILE
    n_tiles, n_chunks = m // ROW_TILE, f // tf
    row_spec = pl.BlockSpec((ROW_TILE, d), lambda i, j: (i, 0))
    in_specs = [row_spec, row_spec,
                pl.BlockSpec((d, tf), lambda i, j: (0, j)),
                pl.BlockSpec((tf, d), lambda i, j: (j, 0)),
                pl.BlockSpec((1, d), lambda i, j: (0, 0))]
    args = [h, x, w_up, w_down, norm_g.reshape(1, d)]
    scratch = [pltpu.VMEM((ROW_TILE, d), F32)]
    if attn_proj is not None:
        w_o, norm_g0 = attn_proj
        in_specs += [pl.BlockSpec((d, d), lambda i, j: (0, 0), pipeline_mode=pl.Buffered(1)),
                     pl.BlockSpec((1, d), lambda i, j: (0, 0))]
        args += [w_o, norm_g0.reshape(1, d)]
        scratch.append(pltpu.VMEM((ROW_TILE, d), BF16))
    out_specs, out_shape = [], []
    if emit_x:
        out_specs.append(row_spec)
        out_shape.append(jax.ShapeDtypeStruct((m, d), F32))
    out_specs.append(row_spec)
    out_shape.append(jax.ShapeDtypeStruct((m, d), norm_dtype))
    if sample_attn is not None:
        q, cache_k, cache_v, layer = sample_attn
        nb, lq, _ = q.shape
        assert nb == n_tiles * n_chunks
        kv_block = (1, 1) + cache_k.shape[2:]
        in_specs += [pl.BlockSpec((1, lq, d), lambda i, j: (i * n_chunks + j, 0, 0)),
                     pl.BlockSpec(kv_block, lambda i, j: (layer, i * n_chunks + j, 0, 0, 0)),
                     pl.BlockSpec(kv_block, lambda i, j: (layer, i * n_chunks + j, 0, 0, 0))]
        args += [q, cache_k, cache_v]
        out_specs.append(pl.BlockSpec((1, lq, d), lambda i, j: (i * n_chunks + j, 0, 0)))
        out_shape.append(jax.ShapeDtypeStruct((nb, lq, d), F32))
    return pl.pallas_call(
        functools.partial(_mlp_kernel, emit_x=emit_x, with_attn=sample_attn is not None,
                          with_proj=attn_proj is not None),
        grid=(n_tiles, n_chunks),
        in_specs=in_specs,
        out_specs=out_specs,
        out_shape=out_shape,
        scratch_shapes=scratch,
        compiler_params=_cparams("arbitrary", "arbitrary"),
        name="mlp",
    )(*args)


def kernel(x_prompt, x_sample, mem_prompt, cache_mem_k, cache_mem_v, state_pool, state_conv,
           norm_mix, w_in, w_pool, pool_scale, conv_w, w_out, norm_attn, norm_mem, w_q, w_k, w_v,
           w_o, norm_mlp, w_up, w_down, norm_final):
    batch, seq, d = x_prompt.shape
    nb, lq, _ = x_sample.shape
    depth = w_in.shape[0]
    pool_buf, pw = state_pool.shape[2], state_pool.shape[3]
    conv_buf, cwid = state_conv.shape[2], state_conv.shape[3]
    n_prompt_rows = batch * seq
    tiles_per_seq = seq // ROW_TILE
    assert nb * lq == ROW_TILE and pool_buf < U_HALO and conv_buf == CONV_K - 1 <= Z_HALO
    assert cache_mem_k.shape[3] == N_XHEADS

    mem_k, mem_k_heads = _mem_proj(mem_prompt, norm_mem, w_k)
    mem_v, mem_v_heads = _mem_proj(mem_prompt, norm_mem, w_v)

    x = [x_prompt.reshape(n_prompt_rows, d), x_sample.transpose(1, 0, 2).reshape(lq * nb, d)]
    h = _rms(x, norm_mix[0])
    pool_p, conv_p, pool_s, conv_s = [], [], [], []
    for l in range(depth):
        sp_tm = state_pool[l].transpose(1, 0, 2)
        sc_tm = state_conv[l].transpose(1, 0, 2)
        ya, yb, u_tail, z_tail, u_s, z_s, w_out_b, w_q_b, w_o_b, w_up_b, w_down_b = _mixin(
            h, w_in, l, sp_tm, sc_tm, w_pool, pool_scale[l], conv_w[l],
            [w_out, w_q, w_o, w_up, w_down], n_prompt_rows=n_prompt_rows, seq=seq)
        x, q = _mm([ya, yb], w_out_b, res=x, norm_g=norm_attn[l], w2=w_q_b)
        last_tiles = slice(tiles_per_seq - 1, batch * tiles_per_seq, tiles_per_seq)
        pool_p.append(u_tail[last_tiles, U_HALO - pool_buf:])
        conv_p.append(z_tail[last_tiles, Z_HALO - conv_buf:])
        pool_s.append(jnp.concatenate([sp_tm, u_s.reshape(lq, nb, pw)], axis=0)[-pool_buf:])
        conv_s.append(z_s.reshape(conv_buf, nb, cwid).transpose(1, 0, 2))
        x_p, h_p = _attn_prompt(q, mem_k_heads, mem_v_heads, l, w_o_b, x, norm_mlp[l],
                                batch=batch, seq=seq)
        q_s = q[n_prompt_rows:].astype(F32).reshape(lq, nb, d).transpose(1, 0, 2)
        last = l == depth - 1
        g_next = norm_final if last else norm_mix[l + 1]
        mlp_opts = dict(norm_dtype=F32 if last else BF16, emit_x=not last)
        *xh_p, o_s = _mlp(h_p, x_p, w_up_b, w_down_b, g_next, **mlp_opts,
                          sample_attn=(q_s, cache_mem_k, cache_mem_v, l))
        o_s = o_s.transpose(1, 0, 2).reshape(lq * nb, d).astype(BF16)
        xh_s = _mlp(o_s, x[n_prompt_rows:], w_up_b, w_down_b, g_next, **mlp_opts,
                    attn_proj=(w_o_b, norm_mlp[l]))
        if last:
            (y_p,), (y_s,) = xh_p, xh_s
        else:
            (x_p, h_p), (x_s, h_s) = xh_p, xh_s
            x, h = [x_p, x_s], [h_p, h_s]

    y_prompt = y_p.reshape(batch, seq, d)
    y_sample = y_s.reshape(lq, nb, d).transpose(1, 0, 2)
    return (y_prompt, y_sample, jnp.stack(pool_p), jnp.stack(conv_p), mem_k, mem_v,
            jnp.stack(pool_s).transpose(0, 2, 1, 3), jnp.stack(conv_s))
```

```python
import functools
import math

import jax
import jax.numpy as jnp
from jax import lax
from jax.experimental import pallas as pl
from jax.experimental.pallas import tpu as pltpu

F32 = jnp.float32
BF16 = jnp.bfloat16

PAST_LEN = 16384
POOL_WINDOWS = (2, 4, 8, 16)
CONV_K = 3
N_XHEADS = 4
EPS = 1e-6

V7X_VMEM_BYTES = 64 * 1024 * 1024
VMEM_LIMIT_BYTES = V7X_VMEM_BYTES - 8 * 1024 * 1024
SUBLANES_F32 = 8
LANES = 128

ROW_TILE = 512


def _cparams(*sem):
    return pltpu.CompilerParams(dimension_semantics=sem, vmem_limit_bytes=VMEM_LIMIT_BYTES)


def _rms_rows(x, g):
    ms = jnp.mean(x * x, axis=-1, keepdims=True)
    return x * lax.rsqrt(ms + EPS) * g


def _parts(x):
    return list(x) if isinstance(x, (list, tuple)) else [x]


def _part_specs(parts, cols, col_index, row_axis):
    specs, start = [], 0
    for p in parts:
        n = p.shape[0] // ROW_TILE
        assert p.shape[0] % ROW_TILE == 0

        def index_map(*ids, start=start, n=n):
            i = ids[row_axis]
            return (jnp.clip(i - start, 0, n - 1), col_index(*ids))

        mode = {"pipeline_mode": pl.Buffered(1)} if n == 1 and len(parts) > 1 else {}
        specs.append(pl.BlockSpec((ROW_TILE, cols), index_map, **mode))
        start += n
    return specs


def _read_parts(refs, i, first_part_tiles):
    if len(refs) == 1:
        return refs[0][...]
    return jnp.where(i < first_part_tiles, refs[0][...], refs[1][...])


def _rms_kernel(*refs, n_parts, first_part_tiles):
    x_refs, g_ref, o_ref = refs[:n_parts], refs[n_parts], refs[n_parts + 1]
    x = _read_parts(x_refs, pl.program_id(0), first_part_tiles)
    o_ref[...] = _rms_rows(x, g_ref[...]).astype(o_ref.dtype)


def _rms(x, g, out_dtype=BF16):
    parts = _parts(x)
    rows = sum(p.shape[0] for p in parts)
    d = parts[0].shape[1]
    return pl.pallas_call(
        functools.partial(_rms_kernel, n_parts=len(parts),
                          first_part_tiles=parts[0].shape[0] // ROW_TILE),
        grid=(rows // ROW_TILE,),
        in_specs=_part_specs(parts, d, lambda i: 0, 0) + [pl.BlockSpec((1, d), lambda i: (0, 0))],
        out_specs=pl.BlockSpec((ROW_TILE, d), lambda i: (i, 0)),
        out_shape=jax.ShapeDtypeStruct((rows, d), out_dtype),
        compiler_params=_cparams("parallel"),
        name="rms",
    )(*parts, g.reshape(1, d))


def _mm_kernel(*refs, kpart_sizes, n_res, chained, first_part_tiles):
    refs = list(refs)
    a_refs = [[refs.pop(0) for _ in range(n)] for n in kpart_sizes]
    w_ref = refs.pop(0)
    res_refs = [refs.pop(0) for _ in range(n_res)]
    g_ref = refs.pop(0)
    w2_ref = refs.pop(0) if chained else None
    o_ref, h_ref = refs
    i = pl.program_id(0)

    acc, k0 = None, 0
    for part_refs in a_refs:
        a = _read_parts(part_refs, i, first_part_tiles)
        t = jnp.dot(a, w_ref[k0:k0 + a.shape[1], :], preferred_element_type=F32)
        acc = t if acc is None else acc + t
        k0 += a.shape[1]
    acc = acc + _read_parts(res_refs, i, first_part_tiles)
    o_ref[...] = acc
    h = _rms_rows(acc, g_ref[...]).astype(BF16)
    if chained:
        h_ref[...] = jnp.dot(h, w2_ref[...], preferred_element_type=F32).astype(h_ref.dtype)
    else:
        h_ref[...] = h


def _mm(a_kparts, w, *, res, norm_g, w2=None):
    a_kparts = [_parts(p) for p in a_kparts]
    res_parts = _parts(res)
    m = sum(p.shape[0] for p in a_kparts[0])
    k, n = w.shape
    assert sum(ps[0].shape[1] for ps in a_kparts) == k
    split = {ps[0].shape[0] // ROW_TILE for ps in a_kparts + [res_parts] if len(ps) == 2}
    assert len(split) <= 1
    first_part_tiles = split.pop() if split else m // ROW_TILE
    resident = dict(pipeline_mode=pl.Buffered(1))
    in_specs, args = [], []
    for ps in a_kparts:
        in_specs += _part_specs(ps, ps[0].shape[1], lambda i: 0, 0)
        args += ps
    in_specs.append(pl.BlockSpec((k, n), lambda i: (0, 0), **resident))
    in_specs += _part_specs(res_parts, n, lambda i: 0, 0)
    in_specs.append(pl.BlockSpec((1, n), lambda i: (0, 0)))
    args += [w] + res_parts + [norm_g.reshape(1, n)]
    n2 = n
    if w2 is not None:
        n2 = w2.shape[1]
        in_specs.append(pl.BlockSpec((n, n2), lambda i: (0, 0), **resident))
        args.append(w2)
    return pl.pallas_call(
        functools.partial(_mm_kernel, kpart_sizes=tuple(len(ps) for ps in a_kparts),
                          n_res=len(res_parts), chained=w2 is not None,
                          first_part_tiles=first_part_tiles),
        grid=(m // ROW_TILE,),
        in_specs=in_specs,
        out_specs=[pl.BlockSpec((ROW_TILE, n), lambda i: (i, 0)),
                   pl.BlockSpec((ROW_TILE, n2), lambda i: (i, 0))],
        out_shape=[jax.ShapeDtypeStruct((m, n), F32), jax.ShapeDtypeStruct((m, n2), BF16)],
        compiler_params=_cparams("parallel"),
        name="mm",
    )(*args)


MEM_ROW_TILE = 256


def _mem_proj_kernel(m_ref, g_ref, w_ref, o_ref, ob_ref, wb_ref):
    @pl.when(pl.program_id(1) == 0)
    def _():
        wb_ref[...] = w_ref[0].astype(BF16)

    h = _rms_rows(m_ref[...], g_ref[0]).astype(BF16)
    kv = jnp.dot(h, wb_ref[...], preferred_element_type=F32)
    hd = o_ref.shape[-1]
    for head in range(N_XHEADS):
        o_ref[0, 0, :, head, :] = kv[:, head * hd:(head + 1) * hd]
        ob_ref[0, 0, head] = kv[:, head * hd:(head + 1) * hd].astype(BF16)


def _mem_proj(mem, norm_mem, w):
    batch, n_mem, d = mem.shape
    depth = w.shape[0]
    tm = MEM_ROW_TILE
    assert n_mem % tm == 0
    tiles_per_seq = n_mem // tm
    hd = d // N_XHEADS
    return pl.pallas_call(
        _mem_proj_kernel,
        grid=(depth, batch * tiles_per_seq),
        in_specs=[pl.BlockSpec((tm, d), lambda l, i: (i, 0)),
                  pl.BlockSpec((1, 1, d), lambda l, i: (l, 0, 0)),
                  pl.BlockSpec((1, d, d), lambda l, i: (l, 0, 0))],
        out_specs=[pl.BlockSpec((1, 1, tm, N_XHEADS, hd),
                                lambda l, i: (l, i // tiles_per_seq, i % tiles_per_seq, 0, 0)),
                   pl.BlockSpec((1, 1, N_XHEADS, tm, hd),
                                lambda l, i: (l, i // tiles_per_seq, 0, i % tiles_per_seq, 0))],
        out_shape=[jax.ShapeDtypeStruct((depth, batch, n_mem, N_XHEADS, hd), F32),
                   jax.ShapeDtypeStruct((depth, batch, N_XHEADS, n_mem, hd), BF16)],
        scratch_shapes=[pltpu.VMEM((d, d), BF16)],
        compiler_params=_cparams("arbitrary", "arbitrary"),
        name="mem_proj",
    )(mem.reshape(batch * n_mem, d), norm_mem.reshape(depth, 1, d), w)


U_HALO = 16
Z_HALO = 8
CAST_STEPS = 64


def _mixin_kernel(*refs, n_h_parts, n_cast, n_prompt_tiles, tiles_per_seq, dec_batch, dec_seq):
    refs = list(refs)
    h_refs = [refs.pop(0) for _ in range(n_h_parts)]
    wu_ref, wb_ref, wc_ref, wv_ref, sp_ref, sc_ref, wp_ref, ps_ref, cw_ref = (
        refs.pop(0) for _ in range(9))
    cast_src = [refs.pop(0) for _ in range(n_cast)]
    ya_ref, yb_ref, ut_ref, zt_ref, np_ref, zs_ref = (refs.pop(0) for _ in range(6))
    cast_dst = [refs.pop(0) for _ in range(n_cast)]
    wcat_ref, hu_ref, hz_ref = refs
    tm = ROW_TILE
    g, i = pl.program_id(0), pl.program_id(1)
    gc = hu_ref.shape[1]
    cw0, cw1, cw2 = cw_ref[0:1, :], cw_ref[1:2, :], cw_ref[2:3, :]

    def for_group(values):
        out = values[-1]
        for gg in range(len(values) - 2, -1, -1):
            out = jnp.where(g == gg, values[gg], out)
        return out

    def project(h_ref):
        for src_ref, dst_ref in zip(cast_src, cast_dst):
            dst_ref[...] = src_ref[0].astype(BF16)
        pj = jnp.dot(h_ref[...], wcat_ref[...], preferred_element_type=F32)
        return pj[:, 0:gc], pj[:, gc:2 * gc], pj[:, 2 * gc:3 * gc], pj[:, 3 * gc:4 * gc]

    def finish(d, yb):
        ya = jnp.dot(d.astype(BF16), wp_ref[0, 0].astype(BF16), preferred_element_type=F32)
        ya_ref[...] = (ya * ps_ref[...]).astype(BF16)
        yb_ref[...] = yb.astype(BF16)

    @pl.when(i == 0)
    def _():
        for k, w_ref in enumerate((wu_ref, wb_ref, wc_ref, wv_ref)):
            wcat_ref[:, k * gc:(k + 1) * gc] = w_ref[0].astype(BF16)

    @pl.when(i < n_prompt_tiles)
    def _prompt():
        start = (i % tiles_per_seq) * tm

        @pl.when(start == 0)
        def _():
            hu_ref[...] = jnp.zeros(hu_ref.shape, F32)
            hz_ref[...] = jnp.zeros(hz_ref.shape, F32)

        u, bg, cg, v = project(h_refs[0])
        ext = jnp.concatenate([hu_ref[...], u], axis=0)
        sums, width = [], 1
        for w in POOL_WINDOWS:
            while width < w:
                ext = ext + pltpu.roll(ext, width, 0)
                width *= 2
            sums.append(ext)
        s = for_group(sums)[U_HALO:]
        w_here = for_group([jnp.int32(w) for w in POOL_WINDOWS])
        inv_w = for_group([jnp.float32(1.0 / w) for w in POOL_WINDOWS])
        pos = start + lax.broadcasted_iota(jnp.int32, (U_HALO, 1), 0)
        count = jnp.minimum(w_here, pos + 1).astype(F32)
        mean = jnp.concatenate([s[:U_HALO] / count, s[U_HALO:] * inv_w], axis=0)
        z = cg * v
        zext = jnp.concatenate([hz_ref[...], z], axis=0)
        c = (cw0 * pltpu.roll(zext, 2, 0)[Z_HALO:] + cw1 * pltpu.roll(zext, 1, 0)[Z_HALO:]
             + cw2 * z)
        finish(mean - u, bg * c)
        ut_ref[0] = u[tm - U_HALO:]
        zt_ref[0] = z[tm - Z_HALO:]
        hu_ref[...] = u[tm - U_HALO:]
        hz_ref[...] = z[tm - Z_HALO:]

    @pl.when(i >= n_prompt_tiles)
    def _sample():
        nb = dec_batch
        n_prev = sp_ref.shape[1]
        u, bg, cg, v = project(h_refs[-1])
        ext = [sp_ref[0, t] for t in range(n_prev)] + [u[l * nb:(l + 1) * nb]
                                                       for l in range(dec_seq)]
        for t in range(n_prev):
            np_ref[t] = ext[t + dec_seq]
        d = []
        for l in range(dec_seq):
            t = n_prev + l
            run, sums = ext[t], []
            for j in range(1, max(POOL_WINDOWS)):
                if j in POOL_WINDOWS:
                    sums.append(run)
                run = run + ext[t - j]
            sums.append(run)
            inv_count = for_group([jnp.float32(1.0 / min(w, PAST_LEN + l + 1))
                                   for w in POOL_WINDOWS])
            d.append(for_group(sums) * inv_count - ext[t])
        z = cg * v
        zext = [sc_ref[0, t] for t in range(sc_ref.shape[1])] + [z[l * nb:(l + 1) * nb]
                                                                 for l in range(dec_seq)]
        c = [cw0 * zext[l] + cw1 * zext[l + 1] + cw2 * zext[l + 2] for l in range(dec_seq)]
        finish(jnp.concatenate(d, axis=0), bg * jnp.concatenate(c, axis=0))
        zs_ref[...] = z[tm - zs_ref.shape[0]:]
        ut_ref[0] = u[tm - U_HALO:]
        zt_ref[0] = jnp.zeros(zt_ref.shape[1:], F32)


def _mixin(h, w_in, layer, state_pool_all, state_conv_all, w_pool, pool_scale, conv_w,
           to_bf16, *, n_prompt_rows, seq):
    h_parts = _parts(h)
    rows, d = sum(p.shape[0] for p in h_parts), h_parts[0].shape[1]
    assert len(h_parts) == 1 or h_parts[0].shape[0] == n_prompt_rows
    tm = ROW_TILE
    n_groups = len(POOL_WINDOWS)
    _, n_pool_prev, dec_batch, pw = state_pool_all.shape
    _, n_conv_prev, _, cwid = state_conv_all.shape
    gc = pw // n_groups
    assert w_in.shape[2] == pw + 3 * cwid and cwid == pw
    n_tiles = rows // tm
    n_prompt_tiles = n_prompt_rows // tm
    dec_seq = (rows - n_prompt_rows) // dec_batch
    assert n_tiles == n_prompt_tiles + 1 and dec_seq * dec_batch == tm and seq % tm == 0
    assert CAST_STEPS <= n_groups * n_tiles
    cast_rows = [w.shape[1] // CAST_STEPS for w in to_bf16]
    assert all(w.shape[1] % CAST_STEPS == 0 and r % (2 * SUBLANES_F32) == 0
               for w, r in zip(to_bf16, cast_rows))

    def chunk(g, i):
        return jnp.minimum(g * n_tiles + i, CAST_STEPS - 1)

    def w_in_spec(k):
        return pl.BlockSpec((1, d, gc), lambda g, i: (layer, 0, k * n_groups + g))

    kernel = functools.partial(_mixin_kernel, n_h_parts=len(h_parts), n_cast=len(to_bf16),
                               n_prompt_tiles=n_prompt_tiles, tiles_per_seq=seq // tm,
                               dec_batch=dec_batch, dec_seq=dec_seq)
    cast_in_specs = [pl.BlockSpec((1, r, w.shape[2]), lambda g, i: (layer, chunk(g, i), 0))
                     for w, r in zip(to_bf16, cast_rows)]
    cast_out_specs = [pl.BlockSpec((r, w.shape[2]), lambda g, i: (chunk(g, i), 0))
                      for w, r in zip(to_bf16, cast_rows)]
    cast_out_shape = [jax.ShapeDtypeStruct(w.shape[1:], BF16) for w in to_bf16]
    return pl.pallas_call(
        kernel,
        grid=(n_groups, n_tiles),
        in_specs=_part_specs(h_parts, d, lambda g, i: 0, 1) + [
            w_in_spec(0), w_in_spec(1), w_in_spec(2), w_in_spec(3),
            pl.BlockSpec((1, n_pool_prev, dec_batch, gc), lambda g, i: (layer, 0, 0, g)),
            pl.BlockSpec((1, n_conv_prev, dec_batch, gc), lambda g, i: (layer, 0, 0, g)),
            pl.BlockSpec((1, 1, gc, gc), lambda g, i: (layer, g, 0, 0)),
            pl.BlockSpec((1, gc), lambda g, i: (0, g)),
            pl.BlockSpec((conv_w.shape[0], gc), lambda g, i: (0, g)),
        ] + cast_in_specs,
        out_specs=[
            pl.BlockSpec((tm, gc), lambda g, i: (i, g)),
            pl.BlockSpec((tm, gc), lambda g, i: (i, g)),
            pl.BlockSpec((1, U_HALO, gc), lambda g, i: (i, 0, g)),
            pl.BlockSpec((1, Z_HALO, gc), lambda g, i: (i, 0, g)),
            pl.BlockSpec((n_pool_prev, dec_batch, gc), lambda g, i: (0, 0, g)),
            pl.BlockSpec((n_conv_prev * dec_batch, gc), lambda g, i: (0, g)),
        ] + cast_out_specs,
        out_shape=[
            jax.ShapeDtypeStruct((rows, pw), BF16),
            jax.ShapeDtypeStruct((rows, cwid), BF16),
            jax.ShapeDtypeStruct((n_tiles, U_HALO, pw), F32),
            jax.ShapeDtypeStruct((n_tiles, Z_HALO, cwid), F32),
            jax.ShapeDtypeStruct((n_pool_prev, dec_batch, pw), F32),
            jax.ShapeDtypeStruct((n_conv_prev * dec_batch, cwid), F32),
        ] + cast_out_shape,
        scratch_shapes=[
            pltpu.VMEM((d, 4 * gc), BF16),
            pltpu.VMEM((U_HALO, gc), F32),
            pltpu.VMEM((Z_HALO, gc), F32),
        ],
        compiler_params=_cparams("arbitrary", "arbitrary"),
        name="mixin",
    )(*h_parts, w_in, w_in, w_in, w_in, state_pool_all, state_conv_all, w_pool,
      pool_scale.reshape(1, pw), conv_w, *to_bf16)


def _masked_softmax_rows(s, valid):
    s = jnp.where(valid, s, -jnp.inf)
    m = jnp.max(s, axis=-1, keepdims=True)
    e = jnp.exp(s - m)
    return e * (1.0 / jnp.sum(e, axis=-1, keepdims=True))


def _softmax_rows(s):
    m = jnp.max(s, axis=-1, keepdims=True)
    e = jnp.exp(s - m)
    return e * (1.0 / jnp.sum(e, axis=-1, keepdims=True))


def _attn_prompt_kernel(q_ref, k_ref, v_ref, wo_ref, x_ref, g_ref, xo_ref, ho_ref,
                        o_even_ref, o_odd_ref, *, n_tiles):
    t = pl.program_id(0)
    hd = k_ref.shape[-1]
    scale = 1.0 / math.sqrt(hd)

    def step(o_new_ref, o_old_ref):
        out = []
        for c in range(N_XHEADS):
            cols = slice(c * hd, (c + 1) * hd)
            if o_new_ref is not None:
                s = lax.dot_general(q_ref[:, cols], k_ref[0, 0, c], (((1,), (1,)), ((), ())),
                                    preferred_element_type=F32) * scale
            if o_old_ref is not None:
                out.append(jnp.dot(o_old_ref[...], wo_ref[:, cols], preferred_element_type=F32))
            if o_new_ref is not None:
                p = _softmax_rows(s).astype(BF16)
                o_new_ref[:, cols] = jnp.dot(p, v_ref[0, 0, c],
                                             preferred_element_type=F32).astype(BF16)
        if o_old_ref is not None:
            acc = x_ref[...] + jnp.concatenate(out, axis=1)
            xo_ref[...] = acc
            ho_ref[...] = _rms_rows(acc, g_ref[...]).astype(ho_ref.dtype)

    last_ref = o_odd_ref if (n_tiles - 1) % 2 else o_even_ref
    pl.when(t == 0)(lambda: step(o_even_ref, None))
    pl.when((t > 0) & (t < n_tiles) & (t % 2 == 1))(lambda: step(o_odd_ref, o_even_ref))
    pl.when((t > 0) & (t < n_tiles) & (t % 2 == 0))(lambda: step(o_even_ref, o_odd_ref))
    pl.when(t == n_tiles)(lambda: step(None, last_ref))


def _attn_prompt(q, mk, mv, layer, w_o, x, norm_g, *, batch, seq):
    d = q.shape[1]
    kv_block = (1, 1) + mk.shape[2:]
    tq = ROW_TILE
    tps = seq // tq
    n_tiles = batch * tps

    def attended(t):
        return jnp.minimum(t, n_tiles - 1)

    def projected(t):
        return jnp.maximum(t - 1, 0)

    out_spec = pl.BlockSpec((tq, d), lambda t: (projected(t), 0))
    return pl.pallas_call(
        functools.partial(_attn_prompt_kernel, n_tiles=n_tiles),
        grid=(n_tiles + 1,),
        in_specs=[pl.BlockSpec((tq, d), lambda t: (attended(t), 0)),
                  pl.BlockSpec(kv_block, lambda t: (layer, attended(t) // tps, 0, 0, 0)),
                  pl.BlockSpec(kv_block, lambda t: (layer, attended(t) // tps, 0, 0, 0)),
                  pl.BlockSpec((d, d), lambda t: (0, 0), pipeline_mode=pl.Buffered(1)),
                  out_spec,
                  pl.BlockSpec((1, d), lambda t: (0, 0))],
        out_specs=[out_spec, out_spec],
        out_shape=[jax.ShapeDtypeStruct((n_tiles * tq, d), F32),
                   jax.ShapeDtypeStruct((n_tiles * tq, d), BF16)],
        scratch_shapes=[pltpu.VMEM((tq, d), BF16), pltpu.VMEM((tq, d), BF16)],
        compiler_params=_cparams("arbitrary"),
        name="attn_prompt",
    )(q, mk, mv, w_o, x, norm_g.reshape(1, d))


def _sample_attn_operands(q, k, v):
    lq, d = q.shape
    n_mem, n_heads, hd = k.shape
    assert lq <= SUBLANES_F32
    q8 = jnp.concatenate([q, jnp.zeros((SUBLANES_F32 - lq, d), F32)], axis=0)
    qh = jnp.concatenate([q8[:, h * hd:(h + 1) * hd] for h in range(n_heads)], axis=0)
    k2 = k.reshape(n_mem * n_heads, hd).astype(BF16)
    v2 = v.reshape(n_mem * n_heads, hd).astype(BF16)
    return qh.astype(BF16), k2, v2


def _sample_attn_probs(qh, k2, n_heads):
    n_rows, hd = qh.shape
    n_keys = k2.shape[0]
    head_of_row = lax.broadcasted_iota(jnp.int32, (n_rows, n_keys), 0) // SUBLANES_F32
    head_of_key = lax.broadcasted_iota(jnp.int32, (n_rows, n_keys), 1) % n_heads
    s = lax.dot_general(qh, k2, (((1,), (1,)), ((), ())),
                        preferred_element_type=F32) * (1.0 / math.sqrt(hd))
    return _masked_softmax_rows(s, head_of_row == head_of_key).astype(BF16)


def _sample_attn_output(p, v2, lq, n_heads):
    o = jnp.dot(p, v2, preferred_element_type=F32)
    return jnp.concatenate(
        [o[h * SUBLANES_F32:h * SUBLANES_F32 + lq] for h in range(n_heads)], axis=1)


def _mlp_kernel(*refs, emit_x, with_attn, with_proj):
    refs = list(refs)
    h_ref, x_ref, wu_ref, wd_ref, g_ref = (refs.pop(0) for _ in range(5))
    wo_ref, g0_ref = (refs.pop(0) for _ in range(2)) if with_proj else (None,) * 2
    q_ref, k_ref, v_ref = (refs.pop(0) for _ in range(3)) if with_attn else (None,) * 3
    xo_ref = refs.pop(0) if emit_x else None
    ho_ref = refs.pop(0)
    os_ref = refs.pop(0) if with_attn else None
    acc_ref = refs.pop(0)
    hs_ref = refs.pop(0) if with_proj else None
    j = pl.program_id(1)

    @pl.when(j == 0)
    def _():
        if with_proj:
            x = x_ref[...] + jnp.dot(h_ref[...], wo_ref[...], preferred_element_type=F32)
            acc_ref[...] = x
            hs_ref[...] = _rms_rows(x, g0_ref[...]).astype(BF16)
        else:
            acc_ref[...] = x_ref[...]

    mlp_in_ref = hs_ref if with_proj else h_ref
    a = jnp.dot(mlp_in_ref[...], wu_ref[...], preferred_element_type=F32)
    if with_attn:
        n_heads = k_ref.shape[3]
        qh, k2, v2 = _sample_attn_operands(q_ref[0], k_ref[0, 0], v_ref[0, 0])
        p = _sample_attn_probs(qh, k2, n_heads)
    a = jnp.square(jnp.maximum(a, 0.0)).astype(BF16)
    acc_ref[...] += jnp.dot(a, wd_ref[...], preferred_element_type=F32)
    if with_attn:
        os_ref[0] = _sample_attn_output(p, v2, q_ref.shape[1], n_heads)

    @pl.when(j == pl.num_programs(1) - 1)
    def _():
        x = acc_ref[...]
        if emit_x:
            xo_ref[...] = x
        ho_ref[...] = _rms_rows(x, g_ref[...]).astype(ho_ref.dtype)


MLP_HIDDEN_TILE = 1024


def _mlp(h, x, w_up, w_down, norm_g, *, norm_dtype=BF16, emit_x=True, sample_attn=None,
         attn_proj=None):
    m, d = x.shape
    f = w_up.shape[1]
    tf = MLP_HIDDEN_TILE
    n_tiles, n_chunks = m // ROW_TILE, f // tf
    row_spec = pl.BlockSpec((ROW_TILE, d), lambda i, j: (i, 0))
    in_specs = [row_spec, row_spec,
                pl.BlockSpec((d, tf), lambda i, j: (0, j)),
                pl.BlockSpec((tf, d), lambda i, j: (j, 0)),
                pl.BlockSpec((1, d), lambda i, j: (0, 0))]
    args = [h, x, w_up, w_down, norm_g.reshape(1, d)]
    scratch = [pltpu.VMEM((ROW_TILE, d), F32)]
    if attn_proj is not None:
        w_o, norm_g0 = attn_proj
        in_specs += [pl.BlockSpec((d, d), lambda i, j: (0, 0), pipeline_mode=pl.Buffered(1)),
                     pl.BlockSpec((1, d), lambda i, j: (0, 0))]
        args += [w_o, norm_g0.reshape(1, d)]
        scratch.append(pltpu.VMEM((ROW_TILE, d), BF16))
    out_specs, out_shape = [], []
    if emit_x:
        out_specs.append(row_spec)
        out_shape.append(jax.ShapeDtypeStruct((m, d), F32))
    out_specs.append(row_spec)
    out_shape.append(jax.ShapeDtypeStruct((m, d), norm_dtype))
    if sample_attn is not None:
        q, cache_k, cache_v, layer = sample_attn
        nb, lq, _ = q.shape
        assert nb == n_tiles * n_chunks
        kv_block = (1, 1) + cache_k.shape[2:]
        in_specs += [pl.BlockSpec((1, lq, d), lambda i, j: (i * n_chunks + j, 0, 0)),
                     pl.BlockSpec(kv_block, lambda i, j: (layer, i * n_chunks + j, 0, 0, 0)),
                     pl.BlockSpec(kv_block, lambda i, j: (layer, i * n_chunks + j, 0, 0, 0))]
        args += [q, cache_k, cache_v]
        out_specs.append(pl.BlockSpec((1, lq, d), lambda i, j: (i * n_chunks + j, 0, 0)))
        out_shape.append(jax.ShapeDtypeStruct((nb, lq, d), F32))
    return pl.pallas_call(
        functools.partial(_mlp_kernel, emit_x=emit_x, with_attn=sample_attn is not None,
                          with_proj=attn_proj is not None),
        grid=(n_tiles, n_chunks),
        in_specs=in_specs,
        out_specs=out_specs,
        out_shape=out_shape,
        scratch_shapes=scratch,
        compiler_params=_cparams("arbitrary", "arbitrary"),
        name="mlp",
    )(*args)


def kernel(x_prompt, x_sample, mem_prompt, cache_mem_k, cache_mem_v, state_pool, state_conv,
           norm_mix, w_in, w_pool, pool_scale, conv_w, w_out, norm_attn, norm_mem, w_q, w_k, w_v,
           w_o, norm_mlp, w_up, w_down, norm_final):
    batch, seq, d = x_prompt.shape
    nb, lq, _ = x_sample.shape
    depth = w_in.shape[0]
    pool_buf, pw = state_pool.shape[2], state_pool.shape[3]
    conv_buf, cwid = state_conv.shape[2], state_conv.shape[3]
    n_prompt_rows = batch * seq
    tiles_per_seq = seq // ROW_TILE
    assert nb * lq == ROW_TILE and pool_buf < U_HALO and conv_buf == CONV_K - 1 <= Z_HALO
    assert cache_mem_k.shape[3] == N_XHEADS

    mem_k, mem_k_heads = _mem_proj(mem_prompt, norm_mem, w_k)
    mem_v, mem_v_heads = _mem_proj(mem_prompt, norm_mem, w_v)

    x = [x_prompt.reshape(n_prompt_rows, d), x_sample.transpose(1, 0, 2).reshape(lq * nb, d)]
    h = _rms(x, norm_mix[0])
    pool_p, conv_p, pool_s, conv_s = [], [], [], []
    sp_all, sc_all = state_pool.transpose(0, 2, 1, 3), state_conv.transpose(0, 2, 1, 3)
    for l in range(depth):
        ya, yb, u_tail, z_tail, new_pool_s, z_s, w_out_b, w_q_b, w_o_b, w_up_b, w_down_b = _mixin(
            h, w_in, l, sp_all, sc_all, w_pool, pool_scale[l], conv_w[l],
            [w_out, w_q, w_o, w_up, w_down], n_prompt_rows=n_prompt_rows, seq=seq)
        x, q = _mm([ya, yb], w_out_b, res=x, norm_g=norm_attn[l], w2=w_q_b)
        last_tiles = slice(tiles_per_seq - 1, batch * tiles_per_seq, tiles_per_seq)
        pool_p.append(u_tail[last_tiles, U_HALO - pool_buf:])
        conv_p.append(z_tail[last_tiles, Z_HALO - conv_buf:])
        pool_s.append(new_pool_s)
        conv_s.append(z_s.reshape(conv_buf, nb, cwid).transpose(1, 0, 2))
        x_p, h_p = _attn_prompt(q, mem_k_heads, mem_v_heads, l, w_o_b, x, norm_mlp[l],
                                batch=batch, seq=seq)
        q_s = q[n_prompt_rows:].astype(F32).reshape(lq, nb, d).transpose(1, 0, 2)
        last = l == depth - 1
        g_next = norm_final if last else norm_mix[l + 1]
        mlp_opts = dict(norm_dtype=F32 if last else BF16, emit_x=not last)
        *xh_p, o_s = _mlp(h_p, x_p, w_up_b, w_down_b, g_next, **mlp_opts,
                          sample_attn=(q_s, cache_mem_k, cache_mem_v, l))
        o_s = o_s.transpose(1, 0, 2).reshape(lq * nb, d).astype(BF16)
        xh_s = _mlp(o_s, x[n_prompt_rows:], w_up_b, w_down_b, g_next, **mlp_opts,
                    attn_proj=(w_o_b, norm_mlp[l]))
        if last:
            (y_p,), (y_s,) = xh_p, xh_s
        else:
            (x_p, h_p), (x_s, h_s) = xh_p, xh_s
            x, h = [x_p, x_s], [h_p, h_s]

    y_prompt = y_p.reshape(batch, seq, d)
    y_sample = y_s.reshape(lq, nb, d).transpose(1, 0, 2)
    return (y_prompt, y_sample, jnp.stack(pool_p), jnp.stack(conv_p), mem_k, mem_v,
            jnp.stack(pool_s).transpose(0, 2, 1, 3), jnp.stack(conv_s))
```

```python
import functools
import math

import jax
import jax.numpy as jnp
from jax import lax
from jax.experimental import pallas as pl
from jax.experimental.pallas import tpu as pltpu

F32 = jnp.float32
BF16 = jnp.bfloat16

PAST_LEN = 16384
POOL_WINDOWS = (2, 4, 8, 16)
CONV_K = 3
N_XHEADS = 4
EPS = 1e-6

V7X_VMEM_BYTES = 64 * 1024 * 1024
VMEM_LIMIT_BYTES = V7X_VMEM_BYTES - 8 * 1024 * 1024
SUBLANES_F32 = 8
LANES = 128

ROW_TILE = 512


def _cparams(*sem):
    return pltpu.CompilerParams(dimension_semantics=sem, vmem_limit_bytes=VMEM_LIMIT_BYTES)


def _rms_rows(x, g):
    ms = jnp.mean(x * x, axis=-1, keepdims=True)
    return x * lax.rsqrt(ms + EPS) * g


def _parts(x):
    return list(x) if isinstance(x, (list, tuple)) else [x]


def _part_specs(parts, cols, col_index, row_axis):
    specs, start = [], 0
    for p in parts:
        n = p.shape[0] // ROW_TILE
        assert p.shape[0] % ROW_TILE == 0

        def index_map(*ids, start=start, n=n):
            i = ids[row_axis]
            return (jnp.clip(i - start, 0, n - 1), col_index(*ids))

        mode = {"pipeline_mode": pl.Buffered(1)} if n == 1 and len(parts) > 1 else {}
        specs.append(pl.BlockSpec((ROW_TILE, cols), index_map, **mode))
        start += n
    return specs


def _read_parts(refs, i, first_part_tiles):
    if len(refs) == 1:
        return refs[0][...]
    return jnp.where(i < first_part_tiles, refs[0][...], refs[1][...])


def _rms_kernel(*refs, n_parts, first_part_tiles):
    x_refs, g_ref, o_ref = refs[:n_parts], refs[n_parts], refs[n_parts + 1]
    x = _read_parts(x_refs, pl.program_id(0), first_part_tiles)
    o_ref[...] = _rms_rows(x, g_ref[...]).astype(o_ref.dtype)


def _rms(x, g, out_dtype=BF16):
    parts = _parts(x)
    rows = sum(p.shape[0] for p in parts)
    d = parts[0].shape[1]
    return pl.pallas_call(
        functools.partial(_rms_kernel, n_parts=len(parts),
                          first_part_tiles=parts[0].shape[0] // ROW_TILE),
        grid=(rows // ROW_TILE,),
        in_specs=_part_specs(parts, d, lambda i: 0, 0) + [pl.BlockSpec((1, d), lambda i: (0, 0))],
        out_specs=pl.BlockSpec((ROW_TILE, d), lambda i: (i, 0)),
        out_shape=jax.ShapeDtypeStruct((rows, d), out_dtype),
        compiler_params=_cparams("parallel"),
        name="rms",
    )(*parts, g.reshape(1, d))


def _mm_kernel(*refs, kpart_sizes, n_res, chained, first_part_tiles):
    refs = list(refs)
    a_refs = [[refs.pop(0) for _ in range(n)] for n in kpart_sizes]
    w_ref = refs.pop(0)
    res_refs = [refs.pop(0) for _ in range(n_res)]
    g_ref = refs.pop(0)
    w2_ref = refs.pop(0) if chained else None
    o_ref, h_ref = refs
    i = pl.program_id(0)

    acc, k0 = None, 0
    for part_refs in a_refs:
        a = _read_parts(part_refs, i, first_part_tiles)
        t = jnp.dot(a, w_ref[k0:k0 + a.shape[1], :], preferred_element_type=F32)
        acc = t if acc is None else acc + t
        k0 += a.shape[1]
    acc = acc + _read_parts(res_refs, i, first_part_tiles)
    o_ref[...] = acc
    h = _rms_rows(acc, g_ref[...]).astype(BF16)
    if chained:
        h_ref[...] = jnp.dot(h, w2_ref[...], preferred_element_type=F32).astype(h_ref.dtype)
    else:
        h_ref[...] = h


def _mm(a_kparts, w, *, res, norm_g, w2=None):
    a_kparts = [_parts(p) for p in a_kparts]
    res_parts = _parts(res)
    m = sum(p.shape[0] for p in a_kparts[0])
    k, n = w.shape
    assert sum(ps[0].shape[1] for ps in a_kparts) == k
    split = {ps[0].shape[0] // ROW_TILE for ps in a_kparts + [res_parts] if len(ps) == 2}
    assert len(split) <= 1
    first_part_tiles = split.pop() if split else m // ROW_TILE
    resident = dict(pipeline_mode=pl.Buffered(1))
    in_specs, args = [], []
    for ps in a_kparts:
        in_specs += _part_specs(ps, ps[0].shape[1], lambda i: 0, 0)
        args += ps
    in_specs.append(pl.BlockSpec((k, n), lambda i: (0, 0), **resident))
    in_specs += _part_specs(res_parts, n, lambda i: 0, 0)
    in_specs.append(pl.BlockSpec((1, n), lambda i: (0, 0)))
    args += [w] + res_parts + [norm_g.reshape(1, n)]
    n2 = n
    if w2 is not None:
        n2 = w2.shape[1]
        in_specs.append(pl.BlockSpec((n, n2), lambda i: (0, 0), **resident))
        args.append(w2)
    return pl.pallas_call(
        functools.partial(_mm_kernel, kpart_sizes=tuple(len(ps) for ps in a_kparts),
                          n_res=len(res_parts), chained=w2 is not None,
                          first_part_tiles=first_part_tiles),
        grid=(m // ROW_TILE,),
        in_specs=in_specs,
        out_specs=[pl.BlockSpec((ROW_TILE, n), lambda i: (i, 0)),
                   pl.BlockSpec((ROW_TILE, n2), lambda i: (i, 0))],
        out_shape=[jax.ShapeDtypeStruct((m, n), F32), jax.ShapeDtypeStruct((m, n2), BF16)],
        compiler_params=_cparams("parallel"),
        name="mm",
    )(*args)


MEM_ROW_TILE = 256


def _mem_proj_kernel(m_ref, g_ref, w_ref, o_ref, ob_ref, wb_ref):
    @pl.when(pl.program_id(1) == 0)
    def _():
        wb_ref[...] = w_ref[0].astype(BF16)

    h = _rms_rows(m_ref[...], g_ref[0]).astype(BF16)
    kv = jnp.dot(h, wb_ref[...], preferred_element_type=F32)
    hd = o_ref.shape[-1]
    for head in range(N_XHEADS):
        o_ref[0, 0, :, head, :] = kv[:, head * hd:(head + 1) * hd]
        ob_ref[0, 0, head] = kv[:, head * hd:(head + 1) * hd].astype(BF16)


def _mem_proj(mem, norm_mem, w):
    batch, n_mem, d = mem.shape
    depth = w.shape[0]
    tm = MEM_ROW_TILE
    assert n_mem % tm == 0
    tiles_per_seq = n_mem // tm
    hd = d // N_XHEADS
    return pl.pallas_call(
        _mem_proj_kernel,
        grid=(depth, batch * tiles_per_seq),
        in_specs=[pl.BlockSpec((tm, d), lambda l, i: (i, 0)),
                  pl.BlockSpec((1, 1, d), lambda l, i: (l, 0, 0)),
                  pl.BlockSpec((1, d, d), lambda l, i: (l, 0, 0))],
        out_specs=[pl.BlockSpec((1, 1, tm, N_XHEADS, hd),
                                lambda l, i: (l, i // tiles_per_seq, i % tiles_per_seq, 0, 0)),
                   pl.BlockSpec((1, 1, N_XHEADS, tm, hd),
                                lambda l, i: (l, i // tiles_per_seq, 0, i % tiles_per_seq, 0))],
        out_shape=[jax.ShapeDtypeStruct((depth, batch, n_mem, N_XHEADS, hd), F32),
                   jax.ShapeDtypeStruct((depth, batch, N_XHEADS, n_mem, hd), BF16)],
        scratch_shapes=[pltpu.VMEM((d, d), BF16)],
        compiler_params=_cparams("arbitrary", "arbitrary"),
        name="mem_proj",
    )(mem.reshape(batch * n_mem, d), norm_mem.reshape(depth, 1, d), w)


U_HALO = 16
Z_HALO = 8
CAST_STEPS = 64


def _mixin_kernel(*refs, n_h_parts, n_cast, n_prompt_tiles, tiles_per_seq, dec_batch, dec_seq):
    refs = list(refs)
    h_refs = [refs.pop(0) for _ in range(n_h_parts)]
    wu_ref, wb_ref, wc_ref, wv_ref, sp_ref, sc_ref, wp_ref, ps_ref, cw_ref = (
        refs.pop(0) for _ in range(9))
    cast_src = [refs.pop(0) for _ in range(n_cast)]
    ya_ref, yb_ref, ut_ref, zt_ref, np_ref, zs_ref = (refs.pop(0) for _ in range(6))
    cast_dst = [refs.pop(0) for _ in range(n_cast)]
    wcat_ref, hu_ref, hz_ref = refs
    tm = ROW_TILE
    g, i = pl.program_id(0), pl.program_id(1)
    gc = hu_ref.shape[1]
    cw0, cw1, cw2 = cw_ref[0:1, :], cw_ref[1:2, :], cw_ref[2:3, :]

    def for_group(values):
        out = values[-1]
        for gg in range(len(values) - 2, -1, -1):
            out = jnp.where(g == gg, values[gg], out)
        return out

    def project(h_ref):
        for src_ref, dst_ref in zip(cast_src, cast_dst):
            dst_ref[...] = src_ref[0].astype(BF16)
        pj = jnp.dot(h_ref[...], wcat_ref[...], preferred_element_type=F32)
        return pj[:, 0:gc], pj[:, gc:2 * gc], pj[:, 2 * gc:3 * gc], pj[:, 3 * gc:4 * gc]

    def finish(d, yb):
        ya = jnp.dot(d.astype(BF16), wp_ref[0, 0].astype(BF16), preferred_element_type=F32)
        ya_ref[...] = (ya * ps_ref[...]).astype(BF16)
        yb_ref[...] = yb.astype(BF16)

    @pl.when(i == 0)
    def _():
        for k, w_ref in enumerate((wu_ref, wb_ref, wc_ref, wv_ref)):
            wcat_ref[:, k * gc:(k + 1) * gc] = w_ref[0].astype(BF16)

    @pl.when(i < n_prompt_tiles)
    def _prompt():
        start = (i % tiles_per_seq) * tm

        @pl.when(start == 0)
        def _():
            hu_ref[...] = jnp.zeros(hu_ref.shape, F32)
            hz_ref[...] = jnp.zeros(hz_ref.shape, F32)

        u, bg, cg, v = project(h_refs[0])
        ext = jnp.concatenate([hu_ref[...], u], axis=0)
        sums, width = [], 1
        for w in POOL_WINDOWS:
            while width < w:
                ext = ext + pltpu.roll(ext, width, 0)
                width *= 2
            sums.append(ext)
        s = for_group(sums)[U_HALO:]
        w_here = for_group([jnp.int32(w) for w in POOL_WINDOWS])
        inv_w = for_group([jnp.float32(1.0 / w) for w in POOL_WINDOWS])
        pos = start + lax.broadcasted_iota(jnp.int32, (U_HALO, 1), 0)
        count = jnp.minimum(w_here, pos + 1).astype(F32)
        mean = jnp.concatenate([s[:U_HALO] / count, s[U_HALO:] * inv_w], axis=0)
        z = cg * v
        zext = jnp.concatenate([hz_ref[...], z], axis=0)
        c = (cw0 * pltpu.roll(zext, 2, 0)[Z_HALO:] + cw1 * pltpu.roll(zext, 1, 0)[Z_HALO:]
             + cw2 * z)
        finish(mean - u, bg * c)
        ut_ref[0] = u[tm - U_HALO:]
        zt_ref[0] = z[tm - Z_HALO:]
        hu_ref[...] = u[tm - U_HALO:]
        hz_ref[...] = z[tm - Z_HALO:]

    @pl.when(i >= n_prompt_tiles)
    def _sample():
        nb = dec_batch
        n_prev = sp_ref.shape[1]
        u, bg, cg, v = project(h_refs[-1])
        ext = [sp_ref[0, t] for t in range(n_prev)] + [u[l * nb:(l + 1) * nb]
                                                       for l in range(dec_seq)]
        for t in range(n_prev):
            np_ref[t] = ext[t + dec_seq]
        d = []
        for l in range(dec_seq):
            t = n_prev + l
            run, sums = ext[t], []
            for j in range(1, max(POOL_WINDOWS)):
                if j in POOL_WINDOWS:
                    sums.append(run)
                run = run + ext[t - j]
            sums.append(run)
            inv_count = for_group([jnp.float32(1.0 / min(w, PAST_LEN + l + 1))
                                   for w in POOL_WINDOWS])
            d.append(for_group(sums) * inv_count - ext[t])
        z = cg * v
        zext = [sc_ref[0, t] for t in range(sc_ref.shape[1])] + [z[l * nb:(l + 1) * nb]
                                                                 for l in range(dec_seq)]
        c = [cw0 * zext[l] + cw1 * zext[l + 1] + cw2 * zext[l + 2] for l in range(dec_seq)]
        finish(jnp.concatenate(d, axis=0), bg * jnp.concatenate(c, axis=0))
        zs_ref[...] = z[tm - zs_ref.shape[0]:]
        ut_ref[0] = u[tm - U_HALO:]
        zt_ref[0] = jnp.zeros(zt_ref.shape[1:], F32)


def _mixin(h, w_in, layer, state_pool_all, state_conv_all, w_pool, pool_scale, conv_w,
           to_bf16, *, n_prompt_rows, seq):
    h_parts = _parts(h)
    rows, d = sum(p.shape[0] for p in h_parts), h_parts[0].shape[1]
    assert len(h_parts) == 1 or h_parts[0].shape[0] == n_prompt_rows
    tm = ROW_TILE
    n_groups = len(POOL_WINDOWS)
    _, n_pool_prev, dec_batch, pw = state_pool_all.shape
    _, n_conv_prev, _, cwid = state_conv_all.shape
    gc = pw // n_groups
    assert w_in.shape[2] == pw + 3 * cwid and cwid == pw
    n_tiles = rows // tm
    n_prompt_tiles = n_prompt_rows // tm
    dec_seq = (rows - n_prompt_rows) // dec_batch
    assert n_tiles == n_prompt_tiles + 1 and dec_seq * dec_batch == tm and seq % tm == 0
    assert CAST_STEPS <= n_groups * n_tiles
    cast_rows = [w.shape[1] // CAST_STEPS for w in to_bf16]
    assert all(w.shape[1] % CAST_STEPS == 0 and r % (2 * SUBLANES_F32) == 0
               for w, r in zip(to_bf16, cast_rows))

    def chunk(g, i):
        return jnp.minimum(g * n_tiles + i, CAST_STEPS - 1)

    def w_in_spec(k):
        return pl.BlockSpec((1, d, gc), lambda g, i: (layer, 0, k * n_groups + g))

    kernel = functools.partial(_mixin_kernel, n_h_parts=len(h_parts), n_cast=len(to_bf16),
                               n_prompt_tiles=n_prompt_tiles, tiles_per_seq=seq // tm,
                               dec_batch=dec_batch, dec_seq=dec_seq)
    cast_in_specs = [pl.BlockSpec((1, r, w.shape[2]), lambda g, i: (layer, chunk(g, i), 0))
                     for w, r in zip(to_bf16, cast_rows)]
    cast_out_specs = [pl.BlockSpec((r, w.shape[2]), lambda g, i: (chunk(g, i), 0))
                      for w, r in zip(to_bf16, cast_rows)]
    cast_out_shape = [jax.ShapeDtypeStruct(w.shape[1:], BF16) for w in to_bf16]
    return pl.pallas_call(
        kernel,
        grid=(n_groups, n_tiles),
        in_specs=_part_specs(h_parts, d, lambda g, i: 0, 1) + [
            w_in_spec(0), w_in_spec(1), w_in_spec(2), w_in_spec(3),
            pl.BlockSpec((1, n_pool_prev, dec_batch, gc), lambda g, i: (layer, 0, 0, g)),
            pl.BlockSpec((1, n_conv_prev, dec_batch, gc), lambda g, i: (layer, 0, 0, g)),
            pl.BlockSpec((1, 1, gc, gc), lambda g, i: (layer, g, 0, 0)),
            pl.BlockSpec((1, gc), lambda g, i: (0, g)),
            pl.BlockSpec((conv_w.shape[0], gc), lambda g, i: (0, g)),
        ] + cast_in_specs,
        out_specs=[
            pl.BlockSpec((tm, gc), lambda g, i: (i, g)),
            pl.BlockSpec((tm, gc), lambda g, i: (i, g)),
            pl.BlockSpec((1, U_HALO, gc), lambda g, i: (i, 0, g)),
            pl.BlockSpec((1, Z_HALO, gc), lambda g, i: (i, 0, g)),
            pl.BlockSpec((n_pool_prev, dec_batch, gc), lambda g, i: (0, 0, g)),
            pl.BlockSpec((n_conv_prev * dec_batch, gc), lambda g, i: (0, g)),
        ] + cast_out_specs,
        out_shape=[
            jax.ShapeDtypeStruct((rows, pw), BF16),
            jax.ShapeDtypeStruct((rows, cwid), BF16),
            jax.ShapeDtypeStruct((n_tiles, U_HALO, pw), F32),
            jax.ShapeDtypeStruct((n_tiles, Z_HALO, cwid), F32),
            jax.ShapeDtypeStruct((n_pool_prev, dec_batch, pw), F32),
            jax.ShapeDtypeStruct((n_conv_prev * dec_batch, cwid), F32),
        ] + cast_out_shape,
        scratch_shapes=[
            pltpu.VMEM((d, 4 * gc), BF16),
            pltpu.VMEM((U_HALO, gc), F32),
            pltpu.VMEM((Z_HALO, gc), F32),
        ],
        compiler_params=_cparams("arbitrary", "arbitrary"),
        name="mixin",
    )(*h_parts, w_in, w_in, w_in, w_in, state_pool_all, state_conv_all, w_pool,
      pool_scale.reshape(1, pw), conv_w, *to_bf16)


def _masked_softmax_rows(s, valid):
    s = jnp.where(valid, s, -jnp.inf)
    m = jnp.max(s, axis=-1, keepdims=True)
    e = jnp.exp(s - m)
    return e * (1.0 / jnp.sum(e, axis=-1, keepdims=True))


def _softmax_rows(s):
    m = jnp.max(s, axis=-1, keepdims=True)
    e = jnp.exp(s - m)
    return e * (1.0 / jnp.sum(e, axis=-1, keepdims=True))


def _attn_prompt_kernel(q_ref, k_ref, v_ref, wo_ref, x_ref, g_ref, xo_ref, ho_ref,
                        o_even_ref, o_odd_ref, *, n_tiles):
    t = pl.program_id(0)
    hd = k_ref.shape[-1]
    scale = 1.0 / math.sqrt(hd)

    def step(o_new_ref, o_old_ref):
        out = []
        for c in range(N_XHEADS):
            cols = slice(c * hd, (c + 1) * hd)
            if o_new_ref is not None:
                s = lax.dot_general(q_ref[:, cols], k_ref[0, 0, c], (((1,), (1,)), ((), ())),
                                    preferred_element_type=F32) * scale
            if o_old_ref is not None:
                out.append(jnp.dot(o_old_ref[...], wo_ref[:, cols], preferred_element_type=F32))
            if o_new_ref is not None:
                p = _softmax_rows(s).astype(BF16)
                o_new_ref[:, cols] = jnp.dot(p, v_ref[0, 0, c],
                                             preferred_element_type=F32).astype(BF16)
        if o_old_ref is not None:
            acc = x_ref[...] + jnp.concatenate(out, axis=1)
            xo_ref[...] = acc
            ho_ref[...] = _rms_rows(acc, g_ref[...]).astype(ho_ref.dtype)

    last_ref = o_odd_ref if (n_tiles - 1) % 2 else o_even_ref
    pl.when(t == 0)(lambda: step(o_even_ref, None))
    pl.when((t > 0) & (t < n_tiles) & (t % 2 == 1))(lambda: step(o_odd_ref, o_even_ref))
    pl.when((t > 0) & (t < n_tiles) & (t % 2 == 0))(lambda: step(o_even_ref, o_odd_ref))
    pl.when(t == n_tiles)(lambda: step(None, last_ref))


def _attn_prompt(q, mk, mv, layer, w_o, x, norm_g, *, batch, seq):
    d = q.shape[1]
    kv_block = (1, 1) + mk.shape[2:]
    tq = ROW_TILE
    tps = seq // tq
    n_tiles = batch * tps

    def attended(t):
        return jnp.minimum(t, n_tiles - 1)

    def projected(t):
        return jnp.maximum(t - 1, 0)

    out_spec = pl.BlockSpec((tq, d), lambda t: (projected(t), 0))
    return pl.pallas_call(
        functools.partial(_attn_prompt_kernel, n_tiles=n_tiles),
        grid=(n_tiles + 1,),
        in_specs=[pl.BlockSpec((tq, d), lambda t: (attended(t), 0)),
                  pl.BlockSpec(kv_block, lambda t: (layer, attended(t) // tps, 0, 0, 0)),
                  pl.BlockSpec(kv_block, lambda t: (layer, attended(t) // tps, 0, 0, 0)),
                  pl.BlockSpec((d, d), lambda t: (0, 0), pipeline_mode=pl.Buffered(1)),
                  out_spec,
                  pl.BlockSpec((1, d), lambda t: (0, 0))],
        out_specs=[out_spec, out_spec],
        out_shape=[jax.ShapeDtypeStruct((n_tiles * tq, d), F32),
                   jax.ShapeDtypeStruct((n_tiles * tq, d), BF16)],
        scratch_shapes=[pltpu.VMEM((tq, d), BF16), pltpu.VMEM((tq, d), BF16)],
        compiler_params=_cparams("arbitrary"),
        name="attn_prompt",
    )(q, mk, mv, w_o, x, norm_g.reshape(1, d))


def _sample_attn_operands(q, k, v):
    lq, d = q.shape
    n_mem, n_heads, hd = k.shape
    assert lq <= SUBLANES_F32
    q8 = jnp.concatenate([q, jnp.zeros((SUBLANES_F32 - lq, d), F32)], axis=0)
    qh = jnp.concatenate([q8[:, h * hd:(h + 1) * hd] for h in range(n_heads)], axis=0)
    k2 = k.reshape(n_mem * n_heads, hd).astype(BF16)
    v2 = v.reshape(n_mem * n_heads, hd).astype(BF16)
    return qh.astype(BF16), k2, v2


def _sample_attn_probs(qh, k2, n_heads):
    n_rows, hd = qh.shape
    n_keys = k2.shape[0]
    head_of_row = lax.broadcasted_iota(jnp.int32, (n_rows, n_keys), 0) // SUBLANES_F32
    head_of_key = lax.broadcasted_iota(jnp.int32, (n_rows, n_keys), 1) % n_heads
    s = lax.dot_general(qh, k2, (((1,), (1,)), ((), ())),
                        preferred_element_type=F32) * (1.0 / math.sqrt(hd))
    return _masked_softmax_rows(s, head_of_row == head_of_key).astype(BF16)


def _sample_attn_output(p, v2, lq, n_heads):
    o = jnp.dot(p, v2, preferred_element_type=F32)
    return jnp.concatenate(
        [o[h * SUBLANES_F32:h * SUBLANES_F32 + lq] for h in range(n_heads)], axis=1)


def _mlp_kernel(*refs, emit_x, with_attn, with_proj):
    refs = list(refs)
    h_ref, x_ref, wu_ref, wd_ref, g_ref = (refs.pop(0) for _ in range(5))
    wo_ref, g0_ref = (refs.pop(0) for _ in range(2)) if with_proj else (None,) * 2
    q_ref, k_ref, v_ref = (refs.pop(0) for _ in range(3)) if with_attn else (None,) * 3
    xo_ref = refs.pop(0) if emit_x else None
    ho_ref = refs.pop(0)
    os_ref = refs.pop(0) if with_attn else None
    acc_ref = refs.pop(0)
    hs_ref = refs.pop(0) if with_proj else None
    j = pl.program_id(1)

    @pl.when(j == 0)
    def _():
        if with_proj:
            x = x_ref[...] + jnp.dot(h_ref[...], wo_ref[...], preferred_element_type=F32)
            acc_ref[...] = x
            hs_ref[...] = _rms_rows(x, g0_ref[...]).astype(BF16)
        else:
            acc_ref[...] = x_ref[...]

    mlp_in_ref = hs_ref if with_proj else h_ref
    a = jnp.dot(mlp_in_ref[...], wu_ref[...], preferred_element_type=F32)
    if with_attn:
        n_heads = k_ref.shape[3]
        qh, k2, v2 = _sample_attn_operands(q_ref[0], k_ref[0, 0], v_ref[0, 0])
        p = _sample_attn_probs(qh, k2, n_heads)
    a = jnp.square(jnp.maximum(a, 0.0)).astype(BF16)
    acc_ref[...] += jnp.dot(a, wd_ref[...], preferred_element_type=F32)
    if with_attn:
        os_ref[0] = _sample_attn_output(p, v2, q_ref.shape[1], n_heads)

    @pl.when(j == pl.num_programs(1) - 1)
    def _():
        x = acc_ref[...]
        if emit_x:
            xo_ref[...] = x
        ho_ref[...] = _rms_rows(x, g_ref[...]).astype(ho_ref.dtype)


MLP_HIDDEN_TILE = 1024


def _mlp(h, x, w_up, w_down, norm_g, *, norm_dtype=BF16, emit_x=True, sample_attn=None,
         attn_proj=None):
    m, d = h.shape
    f = w_up.shape[1]
    tf = MLP_HIDDEN_TILE
    n_tiles, n_chunks = m // ROW_TILE, f // tf
    x_first_tile = (x.shape[0] - m) // ROW_TILE
    row_spec = pl.BlockSpec((ROW_TILE, d), lambda i, j: (i, 0))
    in_specs = [row_spec, pl.BlockSpec((ROW_TILE, d), lambda i, j: (x_first_tile + i, 0)),
                pl.BlockSpec((d, tf), lambda i, j: (0, j)),
                pl.BlockSpec((tf, d), lambda i, j: (j, 0)),
                pl.BlockSpec((1, d), lambda i, j: (0, 0))]
    args = [h, x, w_up, w_down, norm_g.reshape(1, d)]
    scratch = [pltpu.VMEM((ROW_TILE, d), F32)]
    if attn_proj is not None:
        w_o, norm_g0 = attn_proj
        in_specs += [pl.BlockSpec((d, d), lambda i, j: (0, 0), pipeline_mode=pl.Buffered(1)),
                     pl.BlockSpec((1, d), lambda i, j: (0, 0))]
        args += [w_o, norm_g0.reshape(1, d)]
        scratch.append(pltpu.VMEM((ROW_TILE, d), BF16))
    out_specs, out_shape = [], []
    if emit_x:
        out_specs.append(row_spec)
        out_shape.append(jax.ShapeDtypeStruct((m, d), F32))
    out_specs.append(row_spec)
    out_shape.append(jax.ShapeDtypeStruct((m, d), norm_dtype))
    if sample_attn is not None:
        q, cache_k, cache_v, layer = sample_attn
        nb, lq, _ = q.shape
        assert nb == n_tiles * n_chunks
        kv_block = (1, 1) + cache_k.shape[2:]
        in_specs += [pl.BlockSpec((1, lq, d), lambda i, j: (i * n_chunks + j, 0, 0)),
                     pl.BlockSpec(kv_block, lambda i, j: (layer, i * n_chunks + j, 0, 0, 0)),
                     pl.BlockSpec(kv_block, lambda i, j: (layer, i * n_chunks + j, 0, 0, 0))]
        args += [q, cache_k, cache_v]
        out_specs.append(pl.BlockSpec((1, lq, d), lambda i, j: (i * n_chunks + j, 0, 0)))
        out_shape.append(jax.ShapeDtypeStruct((nb, lq, d), F32))
    return pl.pallas_call(
        functools.partial(_mlp_kernel, emit_x=emit_x, with_attn=sample_attn is not None,
                          with_proj=attn_proj is not None),
        grid=(n_tiles, n_chunks),
        in_specs=in_specs,
        out_specs=out_specs,
        out_shape=out_shape,
        scratch_shapes=scratch,
        compiler_params=_cparams("arbitrary", "arbitrary"),
        name="mlp",
    )(*args)


def kernel(x_prompt, x_sample, mem_prompt, cache_mem_k, cache_mem_v, state_pool, state_conv,
           norm_mix, w_in, w_pool, pool_scale, conv_w, w_out, norm_attn, norm_mem, w_q, w_k, w_v,
           w_o, norm_mlp, w_up, w_down, norm_final):
    batch, seq, d = x_prompt.shape
    nb, lq, _ = x_sample.shape
    depth = w_in.shape[0]
    pool_buf, pw = state_pool.shape[2], state_pool.shape[3]
    conv_buf, cwid = state_conv.shape[2], state_conv.shape[3]
    n_prompt_rows = batch * seq
    tiles_per_seq = seq // ROW_TILE
    assert nb * lq == ROW_TILE and pool_buf < U_HALO and conv_buf == CONV_K - 1 <= Z_HALO
    assert cache_mem_k.shape[3] == N_XHEADS

    mem_k, mem_k_heads = _mem_proj(mem_prompt, norm_mem, w_k)
    mem_v, mem_v_heads = _mem_proj(mem_prompt, norm_mem, w_v)

    x = [x_prompt.reshape(n_prompt_rows, d), x_sample.transpose(1, 0, 2).reshape(lq * nb, d)]
    h = _rms(x, norm_mix[0])
    pool_p, conv_p, pool_s, conv_s = [], [], [], []
    sp_all, sc_all = state_pool.transpose(0, 2, 1, 3), state_conv.transpose(0, 2, 1, 3)
    for l in range(depth):
        ya, yb, u_tail, z_tail, new_pool_s, z_s, w_out_b, w_q_b, w_o_b, w_up_b, w_down_b = _mixin(
            h, w_in, l, sp_all, sc_all, w_pool, pool_scale[l], conv_w[l],
            [w_out, w_q, w_o, w_up, w_down], n_prompt_rows=n_prompt_rows, seq=seq)
        x, q = _mm([ya, yb], w_out_b, res=x, norm_g=norm_attn[l], w2=w_q_b)
        last_tiles = slice(tiles_per_seq - 1, batch * tiles_per_seq, tiles_per_seq)
        pool_p.append(u_tail[last_tiles, U_HALO - pool_buf:])
        conv_p.append(z_tail[last_tiles, Z_HALO - conv_buf:])
        pool_s.append(new_pool_s)
        conv_s.append(z_s.reshape(conv_buf, nb, cwid).transpose(1, 0, 2))
        x_p, h_p = _attn_prompt(q, mem_k_heads, mem_v_heads, l, w_o_b, x, norm_mlp[l],
                                batch=batch, seq=seq)
        q_s = q[n_prompt_rows:].astype(F32).reshape(lq, nb, d).transpose(1, 0, 2)
        last = l == depth - 1
        g_next = norm_final if last else norm_mix[l + 1]
        mlp_opts = dict(norm_dtype=F32 if last else BF16, emit_x=not last)
        *xh_p, o_s = _mlp(h_p, x_p, w_up_b, w_down_b, g_next, **mlp_opts,
                          sample_attn=(q_s, cache_mem_k, cache_mem_v, l))
        o_s = o_s.transpose(1, 0, 2).reshape(lq * nb, d).astype(BF16)
        xh_s = _mlp(o_s, x, w_up_b, w_down_b, g_next, **mlp_opts,
                    attn_proj=(w_o_b, norm_mlp[l]))
        if last:
            (y_p,), (y_s,) = xh_p, xh_s
        else:
            (x_p, h_p), (x_s, h_s) = xh_p, xh_s
            x, h = [x_p, x_s], [h_p, h_s]

    y_prompt = y_p.reshape(batch, seq, d)
    y_sample = y_s.reshape(lq, nb, d).transpose(1, 0, 2)
    return (y_prompt, y_sample, jnp.stack(pool_p), jnp.stack(conv_p), mem_k, mem_v,
            jnp.stack(pool_s).transpose(0, 2, 1, 3), jnp.stack(conv_s))
```

```python
import functools
import math

import jax
import jax.numpy as jnp
from jax import lax
from jax.experimental import pallas as pl
from jax.experimental.pallas import tpu as pltpu

F32 = jnp.float32
BF16 = jnp.bfloat16

PAST_LEN = 16384
POOL_WINDOWS = (2, 4, 8, 16)
CONV_K = 3
N_XHEADS = 4
EPS = 1e-6

V7X_VMEM_BYTES = 64 * 1024 * 1024
VMEM_LIMIT_BYTES = V7X_VMEM_BYTES - 8 * 1024 * 1024
SUBLANES_F32 = 8
LANES = 128

ROW_TILE = 512


def _cparams(*sem):
    return pltpu.CompilerParams(dimension_semantics=sem, vmem_limit_bytes=VMEM_LIMIT_BYTES)


def _rms_rows(x, g):
    ms = jnp.mean(x * x, axis=-1, keepdims=True)
    return x * lax.rsqrt(ms + EPS) * g


def _parts(x):
    return list(x) if isinstance(x, (list, tuple)) else [x]


def _part_specs(parts, cols, col_index, row_axis):
    specs, start = [], 0
    for p in parts:
        n = p.shape[0] // ROW_TILE
        assert p.shape[0] % ROW_TILE == 0

        def index_map(*ids, start=start, n=n):
            i = ids[row_axis]
            return (jnp.clip(i - start, 0, n - 1), col_index(*ids))

        mode = {"pipeline_mode": pl.Buffered(1)} if n == 1 and len(parts) > 1 else {}
        specs.append(pl.BlockSpec((ROW_TILE, cols), index_map, **mode))
        start += n
    return specs


def _read_parts(refs, i, first_part_tiles):
    if len(refs) == 1:
        return refs[0][...]
    return jnp.where(i < first_part_tiles, refs[0][...], refs[1][...])


def _rms_kernel(*refs, n_parts, first_part_tiles):
    x_refs, g_ref, o_ref = refs[:n_parts], refs[n_parts], refs[n_parts + 1]
    x = _read_parts(x_refs, pl.program_id(0), first_part_tiles)
    o_ref[...] = _rms_rows(x, g_ref[...]).astype(o_ref.dtype)


def _rms(x, g, out_dtype=BF16):
    parts = _parts(x)
    rows = sum(p.shape[0] for p in parts)
    d = parts[0].shape[1]
    return pl.pallas_call(
        functools.partial(_rms_kernel, n_parts=len(parts),
                          first_part_tiles=parts[0].shape[0] // ROW_TILE),
        grid=(rows // ROW_TILE,),
        in_specs=_part_specs(parts, d, lambda i: 0, 0) + [pl.BlockSpec((1, d), lambda i: (0, 0))],
        out_specs=pl.BlockSpec((ROW_TILE, d), lambda i: (i, 0)),
        out_shape=jax.ShapeDtypeStruct((rows, d), out_dtype),
        compiler_params=_cparams("parallel"),
        name="rms",
    )(*parts, g.reshape(1, d))


def _mm_kernel(*refs, kpart_sizes, n_res, chained, first_part_tiles):
    refs = list(refs)
    a_refs = [[refs.pop(0) for _ in range(n)] for n in kpart_sizes]
    w_ref = refs.pop(0)
    res_refs = [refs.pop(0) for _ in range(n_res)]
    g_ref = refs.pop(0)
    w2_ref = refs.pop(0) if chained else None
    o_ref, h_ref = refs
    i = pl.program_id(0)

    a = jnp.concatenate([_read_parts(part_refs, i, first_part_tiles) for part_refs in a_refs],
                        axis=1)
    acc = jnp.dot(a, w_ref[...], preferred_element_type=F32)
    acc = acc + _read_parts(res_refs, i, first_part_tiles)
    o_ref[...] = acc
    h = _rms_rows(acc, g_ref[...]).astype(BF16)
    if chained:
        h_ref[...] = jnp.dot(h, w2_ref[...], preferred_element_type=F32).astype(h_ref.dtype)
    else:
        h_ref[...] = h


def _mm(a_kparts, w, *, res, norm_g, w2=None):
    a_kparts = [_parts(p) for p in a_kparts]
    res_parts = _parts(res)
    m = sum(p.shape[0] for p in a_kparts[0])
    k, n = w.shape
    assert sum(ps[0].shape[1] for ps in a_kparts) == k
    split = {ps[0].shape[0] // ROW_TILE for ps in a_kparts + [res_parts] if len(ps) == 2}
    assert len(split) <= 1
    first_part_tiles = split.pop() if split else m // ROW_TILE
    resident = dict(pipeline_mode=pl.Buffered(1))
    in_specs, args = [], []
    for ps in a_kparts:
        in_specs += _part_specs(ps, ps[0].shape[1], lambda i: 0, 0)
        args += ps
    in_specs.append(pl.BlockSpec((k, n), lambda i: (0, 0), **resident))
    in_specs += _part_specs(res_parts, n, lambda i: 0, 0)
    in_specs.append(pl.BlockSpec((1, n), lambda i: (0, 0)))
    args += [w] + res_parts + [norm_g.reshape(1, n)]
    n2 = n
    if w2 is not None:
        n2 = w2.shape[1]
        in_specs.append(pl.BlockSpec((n, n2), lambda i: (0, 0), **resident))
        args.append(w2)
    return pl.pallas_call(
        functools.partial(_mm_kernel, kpart_sizes=tuple(len(ps) for ps in a_kparts),
                          n_res=len(res_parts), chained=w2 is not None,
                          first_part_tiles=first_part_tiles),
        grid=(m // ROW_TILE,),
        in_specs=in_specs,
        out_specs=[pl.BlockSpec((ROW_TILE, n), lambda i: (i, 0)),
                   pl.BlockSpec((ROW_TILE, n2), lambda i: (i, 0))],
        out_shape=[jax.ShapeDtypeStruct((m, n), F32), jax.ShapeDtypeStruct((m, n2), BF16)],
        compiler_params=_cparams("parallel"),
        name="mm",
    )(*args)


MEM_ROW_TILE = 256


def _mem_proj_kernel(m_ref, g_ref, w_ref, o_ref, ob_ref, wb_ref):
    @pl.when(pl.program_id(1) == 0)
    def _():
        wb_ref[...] = w_ref[0].astype(BF16)

    h = _rms_rows(m_ref[...], g_ref[0]).astype(BF16)
    kv = jnp.dot(h, wb_ref[...], preferred_element_type=F32)
    hd = o_ref.shape[-1]
    for head in range(N_XHEADS):
        o_ref[0, 0, :, head, :] = kv[:, head * hd:(head + 1) * hd]
        ob_ref[0, 0, head] = kv[:, head * hd:(head + 1) * hd].astype(BF16)


def _mem_proj(mem, norm_mem, w):
    batch, n_mem, d = mem.shape
    depth = w.shape[0]
    tm = MEM_ROW_TILE
    assert n_mem % tm == 0
    tiles_per_seq = n_mem // tm
    hd = d // N_XHEADS
    return pl.pallas_call(
        _mem_proj_kernel,
        grid=(depth, batch * tiles_per_seq),
        in_specs=[pl.BlockSpec((tm, d), lambda l, i: (i, 0)),
                  pl.BlockSpec((1, 1, d), lambda l, i: (l, 0, 0)),
                  pl.BlockSpec((1, d, d), lambda l, i: (l, 0, 0))],
        out_specs=[pl.BlockSpec((1, 1, tm, N_XHEADS, hd),
                                lambda l, i: (l, i // tiles_per_seq, i % tiles_per_seq, 0, 0)),
                   pl.BlockSpec((1, 1, N_XHEADS, tm, hd),
                                lambda l, i: (l, i // tiles_per_seq, 0, i % tiles_per_seq, 0))],
        out_shape=[jax.ShapeDtypeStruct((depth, batch, n_mem, N_XHEADS, hd), F32),
                   jax.ShapeDtypeStruct((depth, batch, N_XHEADS, n_mem, hd), BF16)],
        scratch_shapes=[pltpu.VMEM((d, d), BF16)],
        compiler_params=_cparams("arbitrary", "arbitrary"),
        name="mem_proj",
    )(mem.reshape(batch * n_mem, d), norm_mem.reshape(depth, 1, d), w)


U_HALO = 16
Z_HALO = 8
CAST_STEPS = 64


def _mixin_kernel(*refs, n_h_parts, n_cast, n_prompt_tiles, tiles_per_seq, dec_batch, dec_seq):
    refs = list(refs)
    h_refs = [refs.pop(0) for _ in range(n_h_parts)]
    wu_ref, wb_ref, wc_ref, wv_ref, sp_ref, sc_ref, wp_ref, ps_ref, cw_ref = (
        refs.pop(0) for _ in range(9))
    cast_src = [refs.pop(0) for _ in range(n_cast)]
    ya_ref, yb_ref, ut_ref, zt_ref, np_ref, zs_ref = (refs.pop(0) for _ in range(6))
    cast_dst = [refs.pop(0) for _ in range(n_cast)]
    wcat_ref, hu_ref, hz_ref = refs
    tm = ROW_TILE
    g, i = pl.program_id(0), pl.program_id(1)
    gc = hu_ref.shape[1]
    cw0, cw1, cw2 = cw_ref[0:1, :], cw_ref[1:2, :], cw_ref[2:3, :]

    def for_group(values):
        out = values[-1]
        for gg in range(len(values) - 2, -1, -1):
            out = jnp.where(g == gg, values[gg], out)
        return out

    def project(h_ref):
        for src_ref, dst_ref in zip(cast_src, cast_dst):
            dst_ref[...] = src_ref[0].astype(BF16)
        pj = jnp.dot(h_ref[...], wcat_ref[...], preferred_element_type=F32)
        return pj[:, 0:gc], pj[:, gc:2 * gc], pj[:, 2 * gc:3 * gc], pj[:, 3 * gc:4 * gc]

    def finish(d, yb):
        ya = jnp.dot(d.astype(BF16), wp_ref[0, 0].astype(BF16), preferred_element_type=F32)
        ya_ref[...] = (ya * ps_ref[...]).astype(BF16)
        yb_ref[...] = yb.astype(BF16)

    @pl.when(i == 0)
    def _():
        for k, w_ref in enumerate((wu_ref, wb_ref, wc_ref, wv_ref)):
            wcat_ref[:, k * gc:(k + 1) * gc] = w_ref[0].astype(BF16)

    @pl.when(i < n_prompt_tiles)
    def _prompt():
        start = (i % tiles_per_seq) * tm

        @pl.when(start == 0)
        def _():
            hu_ref[...] = jnp.zeros(hu_ref.shape, F32)
            hz_ref[...] = jnp.zeros(hz_ref.shape, F32)

        u, bg, cg, v = project(h_refs[0])
        ext = jnp.concatenate([hu_ref[...], u], axis=0)
        sums, width = [], 1
        for w in POOL_WINDOWS:
            while width < w:
                ext = ext + pltpu.roll(ext, width, 0)
                width *= 2
            sums.append(ext)
        s = for_group(sums)[U_HALO:]
        w_here = for_group([jnp.int32(w) for w in POOL_WINDOWS])
        inv_w = for_group([jnp.float32(1.0 / w) for w in POOL_WINDOWS])
        pos = start + lax.broadcasted_iota(jnp.int32, (U_HALO, 1), 0)
        count = jnp.minimum(w_here, pos + 1).astype(F32)
        mean = jnp.concatenate([s[:U_HALO] / count, s[U_HALO:] * inv_w], axis=0)
        z = cg * v
        zext = jnp.concatenate([hz_ref[...], z], axis=0)
        c = (cw0 * pltpu.roll(zext, 2, 0)[Z_HALO:] + cw1 * pltpu.roll(zext, 1, 0)[Z_HALO:]
             + cw2 * z)
        finish(mean - u, bg * c)
        ut_ref[0] = u[tm - U_HALO:]
        zt_ref[0] = z[tm - Z_HALO:]
        hu_ref[...] = u[tm - U_HALO:]
        hz_ref[...] = z[tm - Z_HALO:]

    @pl.when(i >= n_prompt_tiles)
    def _sample():
        nb = dec_batch
        n_prev = sp_ref.shape[1]
        u, bg, cg, v = project(h_refs[-1])
        ext = [sp_ref[0, t] for t in range(n_prev)] + [u[l * nb:(l + 1) * nb]
                                                       for l in range(dec_seq)]
        for t in range(n_prev):
            np_ref[t] = ext[t + dec_seq]
        d = []
        for l in range(dec_seq):
            t = n_prev + l
            run, sums = ext[t], []
            for j in range(1, max(POOL_WINDOWS)):
                if j in POOL_WINDOWS:
                    sums.append(run)
                run = run + ext[t - j]
            sums.append(run)
            inv_count = for_group([jnp.float32(1.0 / min(w, PAST_LEN + l + 1))
                                   for w in POOL_WINDOWS])
            d.append(for_group(sums) * inv_count - ext[t])
        z = cg * v
        zext = [sc_ref[0, t] for t in range(sc_ref.shape[1])] + [z[l * nb:(l + 1) * nb]
                                                                 for l in range(dec_seq)]
        c = [cw0 * zext[l] + cw1 * zext[l + 1] + cw2 * zext[l + 2] for l in range(dec_seq)]
        finish(jnp.concatenate(d, axis=0), bg * jnp.concatenate(c, axis=0))
        zs_ref[...] = z[tm - zs_ref.shape[0]:]
        ut_ref[0] = u[tm - U_HALO:]
        zt_ref[0] = jnp.zeros(zt_ref.shape[1:], F32)


def _mixin(h, w_in, layer, state_pool_all, state_conv_all, w_pool, pool_scale, conv_w,
           to_bf16, *, n_prompt_rows, seq):
    h_parts = _parts(h)
    rows, d = sum(p.shape[0] for p in h_parts), h_parts[0].shape[1]
    assert len(h_parts) == 1 or h_parts[0].shape[0] == n_prompt_rows
    tm = ROW_TILE
    n_groups = len(POOL_WINDOWS)
    _, n_pool_prev, dec_batch, pw = state_pool_all.shape
    _, n_conv_prev, _, cwid = state_conv_all.shape
    gc = pw // n_groups
    assert w_in.shape[2] == pw + 3 * cwid and cwid == pw
    n_tiles = rows // tm
    n_prompt_tiles = n_prompt_rows // tm
    dec_seq = (rows - n_prompt_rows) // dec_batch
    assert n_tiles == n_prompt_tiles + 1 and dec_seq * dec_batch == tm and seq % tm == 0
    assert CAST_STEPS <= n_groups * n_tiles
    cast_rows = [w.shape[1] // CAST_STEPS for w in to_bf16]
    assert all(w.shape[1] % CAST_STEPS == 0 and r % (2 * SUBLANES_F32) == 0
               for w, r in zip(to_bf16, cast_rows))

    def chunk(g, i):
        return jnp.minimum(g * n_tiles + i, CAST_STEPS - 1)

    def w_in_spec(k):
        return pl.BlockSpec((1, d, gc), lambda g, i: (layer, 0, k * n_groups + g))

    kernel = functools.partial(_mixin_kernel, n_h_parts=len(h_parts), n_cast=len(to_bf16),
                               n_prompt_tiles=n_prompt_tiles, tiles_per_seq=seq // tm,
                               dec_batch=dec_batch, dec_seq=dec_seq)
    cast_in_specs = [pl.BlockSpec((1, r, w.shape[2]), lambda g, i: (layer, chunk(g, i), 0))
                     for w, r in zip(to_bf16, cast_rows)]
    cast_out_specs = [pl.BlockSpec((r, w.shape[2]), lambda g, i: (chunk(g, i), 0))
                      for w, r in zip(to_bf16, cast_rows)]
    cast_out_shape = [jax.ShapeDtypeStruct(w.shape[1:], BF16) for w in to_bf16]
    return pl.pallas_call(
        kernel,
        grid=(n_groups, n_tiles),
        in_specs=_part_specs(h_parts, d, lambda g, i: 0, 1) + [
            w_in_spec(0), w_in_spec(1), w_in_spec(2), w_in_spec(3),
            pl.BlockSpec((1, n_pool_prev, dec_batch, gc), lambda g, i: (layer, 0, 0, g)),
            pl.BlockSpec((1, n_conv_prev, dec_batch, gc), lambda g, i: (layer, 0, 0, g)),
            pl.BlockSpec((1, 1, gc, gc), lambda g, i: (layer, g, 0, 0)),
            pl.BlockSpec((1, gc), lambda g, i: (0, g)),
            pl.BlockSpec((conv_w.shape[0], gc), lambda g, i: (0, g)),
        ] + cast_in_specs,
        out_specs=[
            pl.BlockSpec((tm, gc), lambda g, i: (i, g)),
            pl.BlockSpec((tm, gc), lambda g, i: (i, g)),
            pl.BlockSpec((1, U_HALO, gc), lambda g, i: (i, 0, g)),
            pl.BlockSpec((1, Z_HALO, gc), lambda g, i: (i, 0, g)),
            pl.BlockSpec((n_pool_prev, dec_batch, gc), lambda g, i: (0, 0, g)),
            pl.BlockSpec((n_conv_prev * dec_batch, gc), lambda g, i: (0, g)),
        ] + cast_out_specs,
        out_shape=[
            jax.ShapeDtypeStruct((rows, pw), BF16),
            jax.ShapeDtypeStruct((rows, cwid), BF16),
            jax.ShapeDtypeStruct((n_tiles, U_HALO, pw), F32),
            jax.ShapeDtypeStruct((n_tiles, Z_HALO, cwid), F32),
            jax.ShapeDtypeStruct((n_pool_prev, dec_batch, pw), F32),
            jax.ShapeDtypeStruct((n_conv_prev * dec_batch, cwid), F32),
        ] + cast_out_shape,
        scratch_shapes=[
            pltpu.VMEM((d, 4 * gc), BF16),
            pltpu.VMEM((U_HALO, gc), F32),
            pltpu.VMEM((Z_HALO, gc), F32),
        ],
        compiler_params=_cparams("arbitrary", "arbitrary"),
        name="mixin",
    )(*h_parts, w_in, w_in, w_in, w_in, state_pool_all, state_conv_all, w_pool,
      pool_scale.reshape(1, pw), conv_w, *to_bf16)


def _masked_softmax_rows(s, valid):
    s = jnp.where(valid, s, -jnp.inf)
    m = jnp.max(s, axis=-1, keepdims=True)
    e = jnp.exp(s - m)
    return e * (1.0 / jnp.sum(e, axis=-1, keepdims=True))


def _softmax_rows(s):
    m = jnp.max(s, axis=-1, keepdims=True)
    e = jnp.exp(s - m)
    return e * (1.0 / jnp.sum(e, axis=-1, keepdims=True))


def _attn_prompt_kernel(q_ref, k_ref, v_ref, wo_ref, x_ref, g_ref, xo_ref, ho_ref,
                        o_even_ref, o_odd_ref, *, n_tiles):
    t = pl.program_id(0)
    hd = k_ref.shape[-1]
    scale = 1.0 / math.sqrt(hd)

    def step(o_new_ref, o_old_ref):
        out = []
        for c in range(N_XHEADS):
            cols = slice(c * hd, (c + 1) * hd)
            if o_new_ref is not None:
                s = lax.dot_general(q_ref[:, cols], k_ref[0, 0, c], (((1,), (1,)), ((), ())),
                                    preferred_element_type=F32) * scale
            if o_old_ref is not None:
                out.append(jnp.dot(o_old_ref[...], wo_ref[:, cols], preferred_element_type=F32))
            if o_new_ref is not None:
                p = _softmax_rows(s).astype(BF16)
                o_new_ref[:, cols] = jnp.dot(p, v_ref[0, 0, c],
                                             preferred_element_type=F32).astype(BF16)
        if o_old_ref is not None:
            acc = x_ref[...] + jnp.concatenate(out, axis=1)
            xo_ref[...] = acc
            ho_ref[...] = _rms_rows(acc, g_ref[...]).astype(ho_ref.dtype)

    last_ref = o_odd_ref if (n_tiles - 1) % 2 else o_even_ref
    pl.when(t == 0)(lambda: step(o_even_ref, None))
    pl.when((t > 0) & (t < n_tiles) & (t % 2 == 1))(lambda: step(o_odd_ref, o_even_ref))
    pl.when((t > 0) & (t < n_tiles) & (t % 2 == 0))(lambda: step(o_even_ref, o_odd_ref))
    pl.when(t == n_tiles)(lambda: step(None, last_ref))


def _attn_prompt(q, mk, mv, layer, w_o, x, norm_g, *, batch, seq):
    d = q.shape[1]
    kv_block = (1, 1) + mk.shape[2:]
    tq = ROW_TILE
    tps = seq // tq
    n_tiles = batch * tps

    def attended(t):
        return jnp.minimum(t, n_tiles - 1)

    def projected(t):
        return jnp.maximum(t - 1, 0)

    out_spec = pl.BlockSpec((tq, d), lambda t: (projected(t), 0))
    return pl.pallas_call(
        functools.partial(_attn_prompt_kernel, n_tiles=n_tiles),
        grid=(n_tiles + 1,),
        in_specs=[pl.BlockSpec((tq, d), lambda t: (attended(t), 0)),
                  pl.BlockSpec(kv_block, lambda t: (layer, attended(t) // tps, 0, 0, 0)),
                  pl.BlockSpec(kv_block, lambda t: (layer, attended(t) // tps, 0, 0, 0)),
                  pl.BlockSpec((d, d), lambda t: (0, 0), pipeline_mode=pl.Buffered(1)),
                  out_spec,
                  pl.BlockSpec((1, d), lambda t: (0, 0))],
        out_specs=[out_spec, out_spec],
        out_shape=[jax.ShapeDtypeStruct((n_tiles * tq, d), F32),
                   jax.ShapeDtypeStruct((n_tiles * tq, d), BF16)],
        scratch_shapes=[pltpu.VMEM((tq, d), BF16), pltpu.VMEM((tq, d), BF16)],
        compiler_params=_cparams("arbitrary"),
        name="attn_prompt",
    )(q, mk, mv, w_o, x, norm_g.reshape(1, d))


def _sample_attn_operands(q, k, v):
    lq, d = q.shape
    n_mem, n_heads, hd = k.shape
    assert lq <= SUBLANES_F32
    q8 = jnp.concatenate([q, jnp.zeros((SUBLANES_F32 - lq, d), F32)], axis=0)
    qh = jnp.concatenate([q8[:, h * hd:(h + 1) * hd] for h in range(n_heads)], axis=0)
    k2 = k.reshape(n_mem * n_heads, hd).astype(BF16)
    v2 = v.reshape(n_mem * n_heads, hd).astype(BF16)
    return qh.astype(BF16), k2, v2


def _sample_attn_probs(qh, k2, n_heads):
    n_rows, hd = qh.shape
    n_keys = k2.shape[0]
    head_of_row = lax.broadcasted_iota(jnp.int32, (n_rows, n_keys), 0) // SUBLANES_F32
    head_of_key = lax.broadcasted_iota(jnp.int32, (n_rows, n_keys), 1) % n_heads
    s = lax.dot_general(qh, k2, (((1,), (1,)), ((), ())),
                        preferred_element_type=F32) * (1.0 / math.sqrt(hd))
    return _masked_softmax_rows(s, head_of_row == head_of_key).astype(BF16)


def _sample_attn_output(p, v2, lq, n_heads):
    o = jnp.dot(p, v2, preferred_element_type=F32)
    return jnp.concatenate(
        [o[h * SUBLANES_F32:h * SUBLANES_F32 + lq] for h in range(n_heads)], axis=1)


def _mlp_kernel(*refs, emit_x, with_attn, with_proj):
    refs = list(refs)
    h_ref, x_ref, wu_ref, wd_ref, g_ref = (refs.pop(0) for _ in range(5))
    wo_ref, g0_ref = (refs.pop(0) for _ in range(2)) if with_proj else (None,) * 2
    q_ref, k_ref, v_ref = (refs.pop(0) for _ in range(3)) if with_attn else (None,) * 3
    xo_ref = refs.pop(0) if emit_x else None
    ho_ref = refs.pop(0)
    os_ref = refs.pop(0) if with_attn else None
    acc_ref = refs.pop(0)
    hs_ref = refs.pop(0) if with_proj else None
    j = pl.program_id(1)

    @pl.when(j == 0)
    def _():
        if with_proj:
            x = x_ref[...] + jnp.dot(h_ref[...], wo_ref[...], preferred_element_type=F32)
            acc_ref[...] = x
            hs_ref[...] = _rms_rows(x, g0_ref[...]).astype(BF16)
        else:
            acc_ref[...] = x_ref[...]

    mlp_in_ref = hs_ref if with_proj else h_ref
    a = jnp.dot(mlp_in_ref[...], wu_ref[...], preferred_element_type=F32)
    if with_attn:
        n_heads = k_ref.shape[3]
        qh, k2, v2 = _sample_attn_operands(q_ref[0], k_ref[0, 0], v_ref[0, 0])
        p = _sample_attn_probs(qh, k2, n_heads)
    a = jnp.square(jnp.maximum(a, 0.0)).astype(BF16)
    acc_ref[...] += jnp.dot(a, wd_ref[...], preferred_element_type=F32)
    if with_attn:
        os_ref[0] = _sample_attn_output(p, v2, q_ref.shape[1], n_heads)

    @pl.when(j == pl.num_programs(1) - 1)
    def _():
        x = acc_ref[...]
        if emit_x:
            xo_ref[...] = x
        ho_ref[...] = _rms_rows(x, g_ref[...]).astype(ho_ref.dtype)


MLP_HIDDEN_TILE = 1024


def _mlp(h, x, w_up, w_down, norm_g, *, norm_dtype=BF16, emit_x=True, sample_attn=None,
         attn_proj=None):
    m, d = h.shape
    f = w_up.shape[1]
    tf = MLP_HIDDEN_TILE
    n_tiles, n_chunks = m // ROW_TILE, f // tf
    x_first_tile = (x.shape[0] - m) // ROW_TILE
    row_spec = pl.BlockSpec((ROW_TILE, d), lambda i, j: (i, 0))
    in_specs = [row_spec, pl.BlockSpec((ROW_TILE, d), lambda i, j: (x_first_tile + i, 0)),
                pl.BlockSpec((d, tf), lambda i, j: (0, j)),
                pl.BlockSpec((tf, d), lambda i, j: (j, 0)),
                pl.BlockSpec((1, d), lambda i, j: (0, 0))]
    args = [h, x, w_up, w_down, norm_g.reshape(1, d)]
    scratch = [pltpu.VMEM((ROW_TILE, d), F32)]
    if attn_proj is not None:
        w_o, norm_g0 = attn_proj
        in_specs += [pl.BlockSpec((d, d), lambda i, j: (0, 0), pipeline_mode=pl.Buffered(1)),
                     pl.BlockSpec((1, d), lambda i, j: (0, 0))]
        args += [w_o, norm_g0.reshape(1, d)]
        scratch.append(pltpu.VMEM((ROW_TILE, d), BF16))
    out_specs, out_shape = [], []
    if emit_x:
        out_specs.append(row_spec)
        out_shape.append(jax.ShapeDtypeStruct((m, d), F32))
    out_specs.append(row_spec)
    out_shape.append(jax.ShapeDtypeStruct((m, d), norm_dtype))
    if sample_attn is not None:
        q, cache_k, cache_v, layer = sample_attn
        nb, lq, _ = q.shape
        assert nb == n_tiles * n_chunks
        kv_block = (1, 1) + cache_k.shape[2:]
        in_specs += [pl.BlockSpec((1, lq, d), lambda i, j: (i * n_chunks + j, 0, 0)),
                     pl.BlockSpec(kv_block, lambda i, j: (layer, i * n_chunks + j, 0, 0, 0)),
                     pl.BlockSpec(kv_block, lambda i, j: (layer, i * n_chunks + j, 0, 0, 0))]
        args += [q, cache_k, cache_v]
        out_specs.append(pl.BlockSpec((1, lq, d), lambda i, j: (i * n_chunks + j, 0, 0)))
        out_shape.append(jax.ShapeDtypeStruct((nb, lq, d), F32))
    return pl.pallas_call(
        functools.partial(_mlp_kernel, emit_x=emit_x, with_attn=sample_attn is not None,
                          with_proj=attn_proj is not None),
        grid=(n_tiles, n_chunks),
        in_specs=in_specs,
        out_specs=out_specs,
        out_shape=out_shape,
        scratch_shapes=scratch,
        compiler_params=_cparams("arbitrary", "arbitrary"),
        name="mlp",
    )(*args)


def kernel(x_prompt, x_sample, mem_prompt, cache_mem_k, cache_mem_v, state_pool, state_conv,
           norm_mix, w_in, w_pool, pool_scale, conv_w, w_out, norm_attn, norm_mem, w_q, w_k, w_v,
           w_o, norm_mlp, w_up, w_down, norm_final):
    batch, seq, d = x_prompt.shape
    nb, lq, _ = x_sample.shape
    depth = w_in.shape[0]
    pool_buf, pw = state_pool.shape[2], state_pool.shape[3]
    conv_buf, cwid = state_conv.shape[2], state_conv.shape[3]
    n_prompt_rows = batch * seq
    tiles_per_seq = seq // ROW_TILE
    assert nb * lq == ROW_TILE and pool_buf < U_HALO and conv_buf == CONV_K - 1 <= Z_HALO
    assert cache_mem_k.shape[3] == N_XHEADS

    mem_k, mem_k_heads = _mem_proj(mem_prompt, norm_mem, w_k)
    mem_v, mem_v_heads = _mem_proj(mem_prompt, norm_mem, w_v)

    x = [x_prompt.reshape(n_prompt_rows, d), x_sample.transpose(1, 0, 2).reshape(lq * nb, d)]
    h = _rms(x, norm_mix[0])
    pool_p, conv_p, pool_s, conv_s = [], [], [], []
    sp_all, sc_all = state_pool.transpose(0, 2, 1, 3), state_conv.transpose(0, 2, 1, 3)
    for l in range(depth):
        ya, yb, u_tail, z_tail, new_pool_s, z_s, w_out_b, w_q_b, w_o_b, w_up_b, w_down_b = _mixin(
            h, w_in, l, sp_all, sc_all, w_pool, pool_scale[l], conv_w[l],
            [w_out, w_q, w_o, w_up, w_down], n_prompt_rows=n_prompt_rows, seq=seq)
        x, q = _mm([ya, yb], w_out_b, res=x, norm_g=norm_attn[l], w2=w_q_b)
        last_tiles = slice(tiles_per_seq - 1, batch * tiles_per_seq, tiles_per_seq)
        pool_p.append(u_tail[last_tiles, U_HALO - pool_buf:])
        conv_p.append(z_tail[last_tiles, Z_HALO - conv_buf:])
        pool_s.append(new_pool_s)
        conv_s.append(z_s.reshape(conv_buf, nb, cwid).transpose(1, 0, 2))
        x_p, h_p = _attn_prompt(q, mem_k_heads, mem_v_heads, l, w_o_b, x, norm_mlp[l],
                                batch=batch, seq=seq)
        q_s = q[n_prompt_rows:].astype(F32).reshape(lq, nb, d).transpose(1, 0, 2)
        last = l == depth - 1
        g_next = norm_final if last else norm_mix[l + 1]
        mlp_opts = dict(norm_dtype=F32 if last else BF16, emit_x=not last)
        *xh_p, o_s = _mlp(h_p, x_p, w_up_b, w_down_b, g_next, **mlp_opts,
                          sample_attn=(q_s, cache_mem_k, cache_mem_v, l))
        o_s = o_s.transpose(1, 0, 2).reshape(lq * nb, d).astype(BF16)
        xh_s = _mlp(o_s, x, w_up_b, w_down_b, g_next, **mlp_opts,
                    attn_proj=(w_o_b, norm_mlp[l]))
        if last:
            (y_p,), (y_s,) = xh_p, xh_s
        else:
            (x_p, h_p), (x_s, h_s) = xh_p, xh_s
            x, h = [x_p, x_s], [h_p, h_s]

    y_prompt = y_p.reshape(batch, seq, d)
    y_sample = y_s.reshape(lq, nb, d).transpose(1, 0, 2)
    return (y_prompt, y_sample, jnp.stack(pool_p), jnp.stack(conv_p), mem_k, mem_v,
            jnp.stack(pool_s).transpose(0, 2, 1, 3), jnp.stack(conv_s))
```
